```python
import math
import jax, jax.numpy as jnp
from jax import lax
import numpy as np

D_MODEL = 1024
BATCH = 8
SEQ = 8192
DEPTH = 1

SB_HEADS = 8
SB_HEAD_DIM = 64
SB_WIDTH = SB_HEADS * SB_HEAD_DIM
MLA_HEADS = 8
MLA_NOPE_DIM = 64
MLA_ROPE_DIM = 32
MLA_V_DIM = 64
MLA_Q_RANK = 384
MLA_KV_RANK = 256
MLA_WIDTH = MLA_HEADS * MLA_V_DIM
ROPE_THETA = 10000.0
D_FF = -(-(8 * D_MODEL) // (3 * 256)) * 256
N_BRANCHES = 2
BLOCK_Q = 128
EPS = 1e-6

IN_SPLITS = [
    SB_WIDTH,
    SB_WIDTH,
    SB_WIDTH,
    MLA_Q_RANK,
    MLA_KV_RANK,
    MLA_ROPE_DIM,
    N_BRANCHES * D_MODEL,
]
IN_OFFSETS = list(np.cumsum(IN_SPLITS)[:-1])
D_IN = int(sum(IN_SPLITS))

kernel_name = "sandwich_gated_sb_mla_swiglu_block"


def rms_norm(x, g):
    xf = x.astype(jnp.float32)
    y = xf * lax.rsqrt(jnp.mean(xf * xf, axis=-1, keepdims=True) + EPS)
    return (y * g.astype(jnp.float32)).astype(x.dtype)


def rope_tables(positions, dtype):
    inv_freq = ROPE_THETA ** (-jnp.arange(0, MLA_ROPE_DIM, 2, dtype=jnp.float32) / MLA_ROPE_DIM)
    ang = positions.astype(jnp.float32)[..., None] * inv_freq
    return jnp.cos(ang)[:, :, None, :].astype(dtype), jnp.sin(ang)[:, :, None, :].astype(dtype)


def apply_rope(x, cos, sin):
    x1, x2 = jnp.split(x, 2, axis=-1)
    return jnp.concatenate([x1 * cos - x2 * sin, x2 * cos + x1 * sin], axis=-1)


def stick_breaking_block(q_blk, k_pre, v_pre, q_start):
    tq, tk = q_blk.shape[1], k_pre.shape[1]
    z = jnp.einsum('bqhd,bkhd->bhqk', q_blk, k_pre).astype(jnp.float32) / math.sqrt(SB_HEAD_DIM)
    qpos = q_start + jnp.arange(tq)
    kpos = jnp.arange(tk)
    causal = kpos[None, :] < qpos[:, None]
    log_1m_beta = jnp.where(causal, jax.nn.log_sigmoid(-z), 0.0)
    rev_excl = lax.cumsum(log_1m_beta, axis=3, reverse=True) - log_1m_beta
    log_a = jax.nn.log_sigmoid(z) + rev_excl
    a = jnp.where(causal, jnp.exp(log_a), 0.0)
    return jnp.einsum('bhqk,bkhd->bqhd', a.astype(v_pre.dtype), v_pre)


def mla_block(qn_blk, qr_blk, kn_pre, kr_pre, v_pre, q_start):
    tq, tk = qn_blk.shape[1], kn_pre.shape[1]
    s = (jnp.einsum('bqhd,bkhd->bhqk', qn_blk, kn_pre)
         + jnp.einsum('bqhr,bkr->bhqk', qr_blk, kr_pre)).astype(jnp.float32)
    s = s / math.sqrt(MLA_NOPE_DIM + MLA_ROPE_DIM)
    causal = jnp.arange(tk)[None, :] <= (q_start + jnp.arange(tq))[:, None]
    s = jnp.where(causal, s, jnp.finfo(jnp.float32).min)
    p = jax.nn.softmax(s, axis=-1)
    return jnp.einsum('bhqk,bkhd->bqhd', p.astype(v_pre.dtype), v_pre)


def setup_inputs(seed: int = 0) -> dict:
    key = jax.random.key(seed)
    ks = jax.random.split(key, 20)

    def w(k, shape, fan_in):
        return jax.random.normal(k, shape, jnp.float32) * fan_in ** -0.5

    def gain(k, n):
        return 1.0 + 0.02 * jax.random.normal(k, (DEPTH, n), jnp.float32)

    x = jax.random.normal(ks[0], (BATCH, SEQ, D_MODEL), jnp.float32)
    offset = jax.random.randint(ks[1], (BATCH, 1), 0, 4096, dtype=jnp.int32)
    positions = (offset + jnp.arange(SEQ, dtype=jnp.int32)[None, :]).astype(jnp.int32)
    return {
        "x": x,
        "positions": positions,
        "norm_mix_pre": gain(ks[2], D_MODEL),
        "norm_mix_post": gain(ks[3], D_MODEL),
        "w_in": w(ks[4], (DEPTH, D_MODEL, D_IN), D_MODEL),
        "b_gate": 0.02 * jax.random.normal(ks[5], (DEPTH, N_BRANCHES * D_MODEL), jnp.float32),
        "q_norm": gain(ks[6], MLA_Q_RANK),
        "w_uq": w(ks[7], (DEPTH, MLA_Q_RANK, MLA_HEADS * (MLA_NOPE_DIM + MLA_ROPE_DIM)), MLA_Q_RANK),
        "kv_norm": gain(ks[8], MLA_KV_RANK),
        "w_ukv": w(ks[9], (DEPTH, MLA_KV_RANK, MLA_HEADS * (MLA_NOPE_DIM + MLA_V_DIM)), MLA_KV_RANK),
        "w_proj_sb": w(ks[10], (DEPTH, SB_WIDTH, D_MODEL), SB_WIDTH),
        "w_proj_mla": w(ks[11], (DEPTH, MLA_WIDTH, D_MODEL), MLA_WIDTH),
        "w_out": w(ks[12], (DEPTH, D_MODEL, D_MODEL), D_MODEL),
        "norm_ffn_pre": gain(ks[13], D_MODEL),
        "norm_ffn_post": gain(ks[14], D_MODEL),
        "w_gate_up": w(ks[15], (DEPTH, D_MODEL, 2 * D_FF), D_MODEL),
        "w_down": w(ks[16], (DEPTH, D_FF, D_MODEL), D_FF),
    }


def reference(x, positions, norm_mix_pre, norm_mix_post, w_in, b_gate, q_norm, w_uq,
              kv_norm, w_ukv, w_proj_sb, w_proj_mla, w_out, norm_ffn_pre, norm_ffn_post,
              w_gate_up, w_down):
    B, S, _ = x.shape
    n_blocks = S // BLOCK_Q
    cos, sin = rope_tables(positions, x.dtype)

    for l in range(DEPTH):
        h = rms_norm(x, norm_mix_pre[l])
        proj = h @ w_in[l]
        q_sb, k_sb, v_sb, c_q, c_kv, k_rope, gate_logits = jnp.split(proj, IN_OFFSETS, axis=-1)

        q_sb = q_sb.reshape(B, S, SB_HEADS, SB_HEAD_DIM)
        k_sb = k_sb.reshape(B, S, SB_HEADS, SB_HEAD_DIM)
        v_sb = v_sb.reshape(B, S, SB_HEADS, SB_HEAD_DIM)

        q_mla = (rms_norm(c_q, q_norm[l]) @ w_uq[l]).reshape(B, S, MLA_HEADS, MLA_NOPE_DIM + MLA_ROPE_DIM)
        q_nope, q_rope = q_mla[..., :MLA_NOPE_DIM], q_mla[..., MLA_NOPE_DIM:]
        q_rope = apply_rope(q_rope, cos, sin)
        kv = (rms_norm(c_kv, kv_norm[l]) @ w_ukv[l]).reshape(B, S, MLA_HEADS, MLA_NOPE_DIM + MLA_V_DIM)
        k_nope, v_mla = kv[..., :MLA_NOPE_DIM], kv[..., MLA_NOPE_DIM:]
        k_rope = apply_rope(k_rope[:, :, None, :], cos, sin)[:, :, 0, :]

        sb_outs, mla_outs = [], []
        for i in range(n_blocks):
            s0, s1 = i * BLOCK_Q, (i + 1) * BLOCK_Q
            sb_outs.append(stick_breaking_block(q_sb[:, s0:s1], k_sb[:, :s1], v_sb[:, :s1], s0))
            mla_outs.append(mla_block(q_nope[:, s0:s1], q_rope[:, s0:s1], k_nope[:, :s1],
                                      k_rope[:, :s1], v_mla[:, :s1], s0))
        o_sb = jnp.concatenate(sb_outs, axis=1).reshape(B, S, SB_WIDTH)
        o_mla = jnp.concatenate(mla_outs, axis=1).reshape(B, S, MLA_WIDTH)

        gates = jax.nn.sigmoid(gate_logits + b_gate[l]).reshape(B, S, N_BRANCHES, D_MODEL)
        merged = gates[:, :, 0] * (o_sb @ w_proj_sb[l]) + gates[:, :, 1] * (o_mla @ w_proj_mla[l])
        y = merged @ w_out[l]
        x = x + rms_norm(y, norm_mix_post[l])

        h = rms_norm(x, norm_ffn_pre[l])
        g, u = jnp.split(h @ w_gate_up[l], 2, axis=-1)
        f = (jax.nn.silu(g) * u) @ w_down[l]
        x = x + rms_norm(f, norm_ffn_post[l])
    return x
```

```python
import functools
import math

import numpy as np
import jax
import jax.numpy as jnp
from jax import lax
from jax.experimental import pallas as pl
from jax.experimental.pallas import tpu as pltpu

D_MODEL = 1024
HEADS = 8
HEAD_DIM = 64
ROPE_DIM = 32
Q_RANK = 384
KV_RANK = 256
D_FF = 2816
ROPE_THETA = 10000.0
EPS = 1e-6

LANES = 128
PAIR_WIDTH = 2 * HEAD_DIM
TOKEN_BLOCK = 512
Q_BLOCK = 128
K_BLOCK = 256
FF_CHUNK = 1408
VMEM_LIMIT_BYTES = 56 * 1024 * 1024

_F32 = jnp.float32
_BF16 = jnp.bfloat16
_NEG = float(np.finfo(np.float32).min)


def _dot(a, b):
    return jnp.dot(a, b, preferred_element_type=_F32)


def _dot_nt(a, b):
    return lax.dot_general(a, b, (((1,), (1,)), ((), ())), preferred_element_type=_F32)


def _rms(x, g):
    return x * lax.rsqrt(jnp.mean(x * x, axis=-1, keepdims=True) + EPS) * g


def _proj_kernel(x_ref, pos_ref, tab_ref, gpre_ref, wq_ref, wk_ref, wv_ref, wcq_ref, wckr_ref,
                 wgate_ref, bgate_ref, qn_ref, wuq_ref, kvn_ref, wukv_ref,
                 qsb_ref, ksb_ref, vsb_ref, qm_ref, km_ref, vm_ref, gates_ref):
    h = _rms(x_ref[...], gpre_ref[...]).astype(_BF16)
    qsb_ref[...] = (_dot(h, wq_ref[...]) * (1.0 / math.sqrt(HEAD_DIM))).astype(_BF16)
    ksb_ref[...] = _dot(h, wk_ref[...]).astype(_BF16)
    vsb_ref[...] = _dot(h, wv_ref[...]).astype(_BF16)
    gates_ref[...] = jax.nn.sigmoid(_dot(h, wgate_ref[...]) + bgate_ref[...]).astype(_BF16)

    ang = pos_ref[...] * tab_ref[0:1, :]
    cos = jnp.cos(ang)
    sin = jnp.sin(ang)
    s_lo = sin * tab_ref[1:2, :]
    s_hi = sin * tab_ref[2:3, :]

    def rope(v):
        return (v * cos + pltpu.roll(v, LANES - ROPE_DIM // 2, 1) * s_lo
                + pltpu.roll(v, ROPE_DIM // 2, 1) * s_hi)

    cq = _dot(h, wcq_ref[...])
    qm = _dot(_rms(cq, qn_ref[...]).astype(_BF16), wuq_ref[...])
    ckr = _dot(h, wckr_ref[...])
    kv = _dot(_rms(ckr[:, :KV_RANK], kvn_ref[...]).astype(_BF16), wukv_ref[...])
    vm_ref[...] = kv[:, HEADS * LANES:].astype(_BF16)
    kr = rope(ckr[:, KV_RANK:])
    q_scale = 1.0 / math.sqrt(HEAD_DIM + ROPE_DIM)
    for hd in range(HEADS):
        sl = slice(hd * LANES, (hd + 1) * LANES)
        qm_ref[:, sl] = (rope(qm[:, sl]) * q_scale).astype(_BF16)
        km_ref[:, sl] = (kv[:, sl] + kr).astype(_BF16)


def _attn_kernel(qsb_ref, ksb_ref, vsb_ref, qm_ref, km_ref, vm_ref, tri_ref, osb_ref, om_ref, *, tq, tk):
    i = pl.program_id(2)
    q_start = i * tq
    n_full = q_start // tk
    n_tiles = (q_start + tq + tk - 1) // tk

    lane = lax.broadcasted_iota(jnp.int32, (1, PAIR_WIDTH), 1)
    qpos = q_start + lax.broadcasted_iota(jnp.int32, (tq, 1), 0)
    kiota = lax.broadcasted_iota(jnp.int32, (1, tk), 1)
    tri = tri_ref[...]
    q_pair = qsb_ref[0]

    def key_slice(j):
        return pl.ds(pl.multiple_of(j * tk, tk), tk)

    sb_out = []
    mla_out = []
    for hh in range(2):
        in_head = (lane >= hh * HEAD_DIM) & (lane < (hh + 1) * HEAD_DIM)
        qh = jnp.where(in_head, q_pair, jnp.zeros_like(q_pair))

        def sb_tile(j, state, masked):
            acc, carry = state
            ks = ksb_ref[0, key_slice(j), :]
            vs = vsb_ref[0, key_slice(j), :]
            z = _dot_nt(qh, ks)
            sp = jnp.maximum(z, 0.0) + jnp.log(1.0 + jnp.exp(-jnp.abs(z)))
            if masked:
                valid = (j * tk + kiota) < qpos
                sp = jnp.where(valid, sp, 0.0)
            later = _dot(sp.astype(_BF16), tri)
            log_a = (z - sp) - (later + carry)
            a = jnp.exp(log_a)
            if masked:
                a = jnp.where(valid, a, 0.0)
            acc = acc + _dot(a.astype(_BF16), vs)
            carry = carry + (later[:, 0:1] + sp[:, 0:1])
            return acc, carry

        state = (jnp.zeros((tq, PAIR_WIDTH), _F32), jnp.zeros((tq, 1), _F32))
        state = lax.fori_loop(0, n_tiles - n_full,
                              lambda t, s: sb_tile(n_tiles - 1 - t, s, True), state)
        state = lax.fori_loop(0, n_full,
                              lambda t, s: sb_tile(n_full - 1 - t, s, False), state)
        sb_out.append(state[0])

        qm = qm_ref[0, :, hh * LANES:(hh + 1) * LANES]

        def mla_tile(j, state, masked):
            m, l, acc = state
            km = km_ref[0, key_slice(j), hh * LANES:(hh + 1) * LANES]
            vm = vm_ref[0, key_slice(j), :]
            s = _dot_nt(qm, km)
            if masked:
                s = jnp.where((j * tk + kiota) <= qpos, s, _NEG)
            m_new = jnp.maximum(m, jnp.max(s, axis=1, keepdims=True))
            p = jnp.exp(s - m_new)
            alpha = jnp.exp(m - m_new)
            l = alpha * l + jnp.sum(p, axis=1, keepdims=True)
            acc = alpha * acc + _dot(p.astype(_BF16), vm)
            return m_new, l, acc

        state = (jnp.full((tq, 1), _NEG, _F32), jnp.zeros((tq, 1), _F32),
                 jnp.zeros((tq, PAIR_WIDTH), _F32))
        state = lax.fori_loop(0, n_full, lambda j, s: mla_tile(j, s, False), state)
        state = lax.fori_loop(n_full, n_tiles, lambda j, s: mla_tile(j, s, True), state)
        mla_out.append(state[2] / state[1])

    first = lane < HEAD_DIM
    osb_ref[0] = jnp.where(first, sb_out[0], sb_out[1]).astype(_BF16)
    om_ref[0] = jnp.where(first, mla_out[0], mla_out[1]).astype(_BF16)


def _mix_kernel(x_ref, osb_ref, om_ref, gates_ref, wpsb_ref, wpm_ref, wout_ref, gpost_ref, x1_ref):
    ysb = _dot(osb_ref[...], wpsb_ref[...])
    ym = _dot(om_ref[...], wpm_ref[...])
    g = gates_ref[...].astype(_F32)
    merged = g[:, :D_MODEL] * ysb + g[:, D_MODEL:] * ym
    y = _dot(merged.astype(_BF16), wout_ref[...])
    x1_ref[...] = x_ref[...] + _rms(y, gpost_ref[...])


def _ffn_kernel(x1_ref, gpre_ref, wgu_ref, wdown_ref, gpost_ref, out_ref):
    x1 = x1_ref[...]
    h = _rms(x1, gpre_ref[...]).astype(_BF16)
    f = None
    for c in range(D_FF // FF_CHUNK):
        g = _dot(h, wgu_ref[:, c * FF_CHUNK:(c + 1) * FF_CHUNK])
        u = _dot(h, wgu_ref[:, D_FF + c * FF_CHUNK:D_FF + (c + 1) * FF_CHUNK])
        act = (g * jax.nn.sigmoid(g) * u).astype(_BF16)
        part = _dot(act, wdown_ref[c * FF_CHUNK:(c + 1) * FF_CHUNK, :])
        f = part if f is None else f + part
    out_ref[...] = x1 + _rms(f, gpost_ref[...])


def _const_spec(shape):
    return pl.BlockSpec(shape, lambda *_: (0,) * len(shape))


def _row_spec(tm, width):
    return pl.BlockSpec((tm, width), lambda t: (t, 0))


def _rope_table():
    inv_freq = ROPE_THETA ** (-jnp.arange(0, ROPE_DIM, 2, dtype=_F32) / ROPE_DIM)
    half = ROPE_DIM // 2
    tab = jnp.zeros((8, LANES), _F32)
    tab = tab.at[0, HEAD_DIM:HEAD_DIM + half].set(inv_freq)
    tab = tab.at[0, HEAD_DIM + half:HEAD_DIM + ROPE_DIM].set(inv_freq)
    tab = tab.at[1, HEAD_DIM:HEAD_DIM + half].set(-1.0)
    tab = tab.at[2, HEAD_DIM + half:HEAD_DIM + ROPE_DIM].set(1.0)
    return tab


def kernel(x, positions, norm_mix_pre, norm_mix_post, w_in, b_gate, q_norm, w_uq, kv_norm, w_ukv,
           w_proj_sb, w_proj_mla, w_out, norm_ffn_pre, norm_ffn_post, w_gate_up, w_down):
    B, S, _ = x.shape
    T = B * S
    tm = min(TOKEN_BLOCK, T)
    tq, tk = Q_BLOCK, K_BLOCK
    assert w_in.shape[0] == 1, "single-layer block"
    assert T % tm == 0 and S % tq == 0 and S % tk == 0

    sbw = HEADS * HEAD_DIM
    win = w_in[0]
    o_cq = 3 * sbw
    o_ckv = o_cq + Q_RANK
    o_kr = o_ckv + KV_RANK
    o_gate = o_kr + ROPE_DIM
    wq = win[:, 0:sbw].astype(_BF16)
    wk = win[:, sbw:2 * sbw].astype(_BF16)
    wv = win[:, 2 * sbw:3 * sbw].astype(_BF16)
    wcq = win[:, o_cq:o_ckv].astype(_BF16)
    kr_cols = jnp.zeros((D_MODEL, LANES), _F32).at[:, HEAD_DIM:HEAD_DIM + ROPE_DIM].set(win[:, o_kr:o_gate])
    wckr = jnp.concatenate([win[:, o_ckv:o_kr], kr_cols], axis=1).astype(_BF16)
    wgate = win[:, o_gate:].astype(_BF16)
    wuq = jnp.pad(w_uq[0].reshape(Q_RANK, HEADS, HEAD_DIM + ROPE_DIM),
                  ((0, 0), (0, 0), (0, LANES - HEAD_DIM - ROPE_DIM))).reshape(Q_RANK, HEADS * LANES).astype(_BF16)
    wukv3 = w_ukv[0].reshape(KV_RANK, HEADS, 2 * HEAD_DIM)
    wuk = jnp.pad(wukv3[:, :, :HEAD_DIM], ((0, 0), (0, 0), (0, LANES - HEAD_DIM))).reshape(KV_RANK, HEADS * LANES)
    wuv = wukv3[:, :, HEAD_DIM:].reshape(KV_RANK, HEADS * HEAD_DIM)
    wukv = jnp.concatenate([wuk, wuv], axis=1).astype(_BF16)

    x2 = x.reshape(T, D_MODEL)
    pos = positions.reshape(T, 1).astype(_F32)
    row = lambda v: v.reshape(1, -1).astype(_F32)

    params = pltpu.CompilerParams(dimension_semantics=("parallel",), vmem_limit_bytes=VMEM_LIMIT_BYTES)

    proj_out_widths = [sbw, sbw, sbw, HEADS * LANES, HEADS * LANES, sbw, 2 * D_MODEL]
    qsb, ksb, vsb, qm, km, vm, gates = pl.pallas_call(
        _proj_kernel,
        grid=(T // tm,),
        in_specs=[_row_spec(tm, D_MODEL), _row_spec(tm, 1), _const_spec((8, LANES)), _const_spec((1, D_MODEL)),
                  _const_spec((D_MODEL, sbw)), _const_spec((D_MODEL, sbw)), _const_spec((D_MODEL, sbw)),
                  _const_spec((D_MODEL, Q_RANK)), _const_spec((D_MODEL, KV_RANK + LANES)),
                  _const_spec((D_MODEL, 2 * D_MODEL)), _const_spec((1, 2 * D_MODEL)),
                  _const_spec((1, Q_RANK)), _const_spec((Q_RANK, HEADS * LANES)),
                  _const_spec((1, KV_RANK)), _const_spec((KV_RANK, HEADS * LANES + sbw))],
        out_specs=[_row_spec(tm, w) for w in proj_out_widths],
        out_shape=[jax.ShapeDtypeStruct((T, w), _BF16) for w in proj_out_widths],
        compiler_params=params,
        name="proj",
    )(x2, pos, _rope_table(), row(norm_mix_pre[0]), wq, wk, wv, wcq, wckr, wgate, row(b_gate[0]),
      row(q_norm[0]), wuq, row(kv_norm[0]), wukv)

    tri = (lax.broadcasted_iota(jnp.int32, (tk, tk), 0) > lax.broadcasted_iota(jnp.int32, (tk, tk), 1)).astype(_BF16)
    pairs = HEADS // 2
    q_spec = lambda w: pl.BlockSpec((1, tq, w), lambda b, p, i: (b, i, p))
    kv_spec = lambda w: pl.BlockSpec((1, S, w), lambda b, p, i: (b, 0, p))
    osb, om = pl.pallas_call(
        functools.partial(_attn_kernel, tq=tq, tk=tk),
        grid=(B, pairs, S // tq),
        in_specs=[q_spec(PAIR_WIDTH), kv_spec(PAIR_WIDTH), kv_spec(PAIR_WIDTH),
                  q_spec(2 * LANES), kv_spec(2 * LANES), kv_spec(PAIR_WIDTH),
                  pl.BlockSpec((tk, tk), lambda b, p, i: (0, 0))],
        out_specs=[q_spec(PAIR_WIDTH), q_spec(PAIR_WIDTH)],
        out_shape=[jax.ShapeDtypeStruct((B, S, sbw), _BF16)] * 2,
        compiler_params=pltpu.CompilerParams(dimension_semantics=("parallel", "parallel", "arbitrary"),
                                             vmem_limit_bytes=VMEM_LIMIT_BYTES),
        name="attn",
    )(qsb.reshape(B, S, sbw), ksb.reshape(B, S, sbw), vsb.reshape(B, S, sbw),
      qm.reshape(B, S, HEADS * LANES), km.reshape(B, S, HEADS * LANES), vm.reshape(B, S, sbw), tri)

    x1 = pl.pallas_call(
        _mix_kernel,
        grid=(T // tm,),
        in_specs=[_row_spec(tm, D_MODEL), _row_spec(tm, sbw), _row_spec(tm, sbw), _row_spec(tm, 2 * D_MODEL),
                  _const_spec((sbw, D_MODEL)), _const_spec((sbw, D_MODEL)), _const_spec((D_MODEL, D_MODEL)),
                  _const_spec((1, D_MODEL))],
        out_specs=_row_spec(tm, D_MODEL),
        out_shape=jax.ShapeDtypeStruct((T, D_MODEL), _F32),
        compiler_params=params,
        name="mix",
    )(x2, osb.reshape(T, sbw), om.reshape(T, sbw), gates, w_proj_sb[0].astype(_BF16),
      w_proj_mla[0].astype(_BF16), w_out[0].astype(_BF16), row(norm_mix_post[0]))

    out = pl.pallas_call(
        _ffn_kernel,
        grid=(T // tm,),
        in_specs=[_row_spec(tm, D_MODEL), _const_spec((1, D_MODEL)), _const_spec((D_MODEL, 2 * D_FF)),
                  _const_spec((D_FF, D_MODEL)), _const_spec((1, D_MODEL))],
        out_specs=_row_spec(tm, D_MODEL),
        out_shape=jax.ShapeDtypeStruct((T, D_MODEL), _F32),
        compiler_params=params,
        name="ffn",
    )(x1, row(norm_ffn_pre[0]), w_gate_up[0].astype(_BF16), w_down[0].astype(_BF16), row(norm_ffn_post[0]))

    return out.reshape(B, S, D_MODEL)
```

```python
import functools
import math

import numpy as np
import jax
import jax.numpy as jnp
from jax import lax
from jax.experimental import pallas as pl
from jax.experimental.pallas import tpu as pltpu

D_MODEL = 1024
HEADS = 8
HEAD_DIM = 64
ROPE_DIM = 32
Q_RANK = 384
KV_RANK = 256
D_FF = 2816
ROPE_THETA = 10000.0
EPS = 1e-6

LANES = 128
PAIR_WIDTH = 2 * HEAD_DIM
TOKEN_BLOCK = 512
Q_BLOCK = 256
K_BLOCK = 512
CUMSUM_BLOCK = 256
FF_CHUNK = 1408
VMEM_LIMIT_BYTES = 56 * 1024 * 1024

_F32 = jnp.float32
_BF16 = jnp.bfloat16
_NEG = float(np.finfo(np.float32).min)
LOG2E = math.log2(math.e)
_SIGN_BIT = np.uint32(0x80000000)


def _dot(a, b):
    return jnp.dot(a, b, preferred_element_type=_F32)


def _dot_nt(a, b):
    return lax.dot_general(a, b, (((1,), (1,)), ((), ())), preferred_element_type=_F32)


def _rms(x, g):
    return x * lax.rsqrt(jnp.mean(x * x, axis=-1, keepdims=True) + EPS) * g


def _proj_kernel(x_ref, pos_ref, tab_ref, gpre_ref, wq_ref, wk_ref, wv_ref, wcq_ref, wckr_ref,
                 wgate_ref, bgate_ref, qn_ref, wuq_ref, kvn_ref, wukv_ref,
                 qsb_ref, ksb_ref, vsb_ref, qm_ref, km_ref, vm_ref, gates_ref):
    h = _rms(x_ref[...], gpre_ref[...]).astype(_BF16)
    qsb_ref[...] = (_dot(h, wq_ref[...]) * (LOG2E / math.sqrt(HEAD_DIM))).astype(_BF16)
    ksb_ref[...] = _dot(h, wk_ref[...]).astype(_BF16)
    vsb_ref[...] = _dot(h, wv_ref[...]).astype(_BF16)
    gates_ref[...] = jax.nn.sigmoid(_dot(h, wgate_ref[...]) + bgate_ref[...]).astype(_BF16)

    ang = pos_ref[...] * tab_ref[0:1, :]
    cos = jnp.cos(ang)
    sin = jnp.sin(ang)
    s_lo = sin * tab_ref[1:2, :]
    s_hi = sin * tab_ref[2:3, :]

    def rope(v):
        return (v * cos + pltpu.roll(v, LANES - ROPE_DIM // 2, 1) * s_lo
                + pltpu.roll(v, ROPE_DIM // 2, 1) * s_hi)

    cq = _dot(h, wcq_ref[...])
    qm = _dot(_rms(cq, qn_ref[...]).astype(_BF16), wuq_ref[...])
    ckr = _dot(h, wckr_ref[...])
    kv = _dot(_rms(ckr[:, :KV_RANK], kvn_ref[...]).astype(_BF16), wukv_ref[...])
    vm_ref[...] = kv[:, HEADS * LANES:].astype(_BF16)
    kr = rope(ckr[:, KV_RANK:])
    q_scale = LOG2E / math.sqrt(HEAD_DIM + ROPE_DIM)
    for hd in range(HEADS):
        sl = slice(hd * LANES, (hd + 1) * LANES)
        qm_ref[:, sl] = (rope(qm[:, sl]) * q_scale).astype(_BF16)
        km_ref[:, sl] = (kv[:, sl] + kr).astype(_BF16)


def _attn_kernel(qsb_ref, ksb_ref, vsb_ref, qm_ref, km_ref, vm_ref, tri_ref, osb_ref, om_ref,
                 u_scr, spb_scr, s_scr, acc_sb_scr, acc_m_scr, *, tq, tk):
    cs = CUMSUM_BLOCK
    i = pl.program_id(2)
    q_start = i * tq
    n_full = q_start // tk

    lane = lax.broadcasted_iota(jnp.int32, (1, PAIR_WIDTH), 1)
    lane2 = lax.broadcasted_iota(jnp.int32, (1, 2 * LANES), 1)
    row = lax.broadcasted_iota(jnp.int32, (tq, 1), 0)
    qpos = q_start + jnp.concatenate([row, row], axis=0)
    kiota = lax.broadcasted_iota(jnp.int32, (1, tk), 1)
    tri = tri_ref[...]
    q_pair = qsb_ref[0]
    zero = jnp.zeros_like(q_pair)
    q_sb = jnp.concatenate([jnp.where(lane < HEAD_DIM, q_pair, zero),
                            jnp.where(lane >= HEAD_DIM, q_pair, zero)], axis=0)
    qm_pair = qm_ref[0]
    zero2 = jnp.zeros_like(qm_pair)
    q_mla = jnp.concatenate([jnp.where(lane2 < LANES, qm_pair, zero2),
                             jnp.where(lane2 >= LANES, qm_pair, zero2)], axis=0)

    def tile_rows(j):
        return pl.ds(pl.multiple_of(j * tk, tk), tk)

    def score_stage(j, masked):
        rows = tile_rows(j)
        z = _dot_nt(q_sb, ksb_ref[0, rows, :])
        neg_abs = lax.bitcast_convert_type(lax.bitcast_convert_type(z, jnp.uint32) | _SIGN_BIT, _F32)
        sp = jnp.maximum(z, 0.0) + jnp.log(1.0 + jnp.exp2(neg_abs)) * LOG2E
        u = z - sp
        s = _dot_nt(q_mla, km_ref[0, rows, :])
        if masked:
            kpos = j * tk + kiota
            valid = kpos < qpos
            sp = jnp.where(valid, sp, 0.0)
            u = jnp.where(valid, u, _NEG)
            s = jnp.where(kpos <= qpos, s, _NEG)
        u_scr[...] = u
        spb_scr[...] = sp.astype(_BF16)
        s_scr[...] = s

    def value_stage(j, state):
        carry, m, l = state
        rows = tile_rows(j)
        a_parts = []
        for c in reversed(range(tk // cs)):
            spb = spb_scr[:, c * cs:(c + 1) * cs]
            later = _dot(spb, tri)
            a_parts.append(jnp.exp2(u_scr[:, c * cs:(c + 1) * cs] - (later + carry)).astype(_BF16))
            carry = carry + (later[:, 0:1] + spb[:, 0:1].astype(_F32))
        a = jnp.concatenate(a_parts[::-1], axis=1)
        acc_sb_scr[...] += _dot(a, vsb_ref[0, rows, :])

        s = s_scr[...]
        m_new = jnp.maximum(m, jnp.max(s, axis=1, keepdims=True))
        p = jnp.exp2(s - m_new)
        alpha = jnp.exp2(m - m_new)
        l = alpha * l + jnp.sum(p, axis=1, keepdims=True)
        acc_m_scr[...] = alpha * acc_m_scr[...] + _dot(p.astype(_BF16), vm_ref[0, rows, :])
        return carry, m_new, l

    acc_sb_scr[...] = jnp.zeros_like(acc_sb_scr)
    acc_m_scr[...] = jnp.zeros_like(acc_m_scr)
    state = (jnp.zeros((2 * tq, 1), _F32), jnp.full((2 * tq, 1), _NEG, _F32), jnp.zeros((2 * tq, 1), _F32))
    score_stage(n_full, True)

    def step(t, state):
        j = n_full - t
        state = value_stage(j, state)
        score_stage(j - 1, False)
        return state

    state = lax.fori_loop(0, n_full, step, state)
    _, _, l = value_stage(0, state)

    first = lane < HEAD_DIM
    acc_sb = acc_sb_scr[...]
    o_m = acc_m_scr[...] / l
    osb_ref[0] = jnp.where(first, acc_sb[:tq], acc_sb[tq:]).astype(_BF16)
    om_ref[0] = jnp.where(first, o_m[:tq], o_m[tq:]).astype(_BF16)


def _mix_kernel(x_ref, osb_ref, om_ref, gates_ref, wpsb_ref, wpm_ref, wout_ref, gpost_ref, x1_ref):
    ysb = _dot(osb_ref[...], wpsb_ref[...])
    ym = _dot(om_ref[...], wpm_ref[...])
    g = gates_ref[...].astype(_F32)
    merged = g[:, :D_MODEL] * ysb + g[:, D_MODEL:] * ym
    y = _dot(merged.astype(_BF16), wout_ref[...])
    x1_ref[...] = x_ref[...] + _rms(y, gpost_ref[...])


def _ffn_kernel(x1_ref, gpre_ref, wgu_ref, wdown_ref, gpost_ref, out_ref):
    x1 = x1_ref[...]
    h = _rms(x1, gpre_ref[...]).astype(_BF16)
    f = None
    for c in range(D_FF // FF_CHUNK):
        g = _dot(h, wgu_ref[:, c * FF_CHUNK:(c + 1) * FF_CHUNK])
        u = _dot(h, wgu_ref[:, D_FF + c * FF_CHUNK:D_FF + (c + 1) * FF_CHUNK])
        act = (g * jax.nn.sigmoid(g) * u).astype(_BF16)
        part = _dot(act, wdown_ref[c * FF_CHUNK:(c + 1) * FF_CHUNK, :])
        f = part if f is None else f + part
    out_ref[...] = x1 + _rms(f, gpost_ref[...])


def _const_spec(shape):
    return pl.BlockSpec(shape, lambda *_: (0,) * len(shape))


def _row_spec(tm, width):
    return pl.BlockSpec((tm, width), lambda t: (t, 0))


def _rope_table():
    inv_freq = ROPE_THETA ** (-jnp.arange(0, ROPE_DIM, 2, dtype=_F32) / ROPE_DIM)
    half = ROPE_DIM // 2
    tab = jnp.zeros((8, LANES), _F32)
    tab = tab.at[0, HEAD_DIM:HEAD_DIM + half].set(inv_freq)
    tab = tab.at[0, HEAD_DIM + half:HEAD_DIM + ROPE_DIM].set(inv_freq)
    tab = tab.at[1, HEAD_DIM:HEAD_DIM + half].set(-1.0)
    tab = tab.at[2, HEAD_DIM + half:HEAD_DIM + ROPE_DIM].set(1.0)
    return tab


def kernel(x, positions, norm_mix_pre, norm_mix_post, w_in, b_gate, q_norm, w_uq, kv_norm, w_ukv,
           w_proj_sb, w_proj_mla, w_out, norm_ffn_pre, norm_ffn_post, w_gate_up, w_down):
    B, S, _ = x.shape
    T = B * S
    tm = min(TOKEN_BLOCK, T)
    tq, tk = Q_BLOCK, K_BLOCK
    assert w_in.shape[0] == 1, "single-layer block"
    assert T % tm == 0 and S % tk == 0 and tk % tq == 0
    assert tk % CUMSUM_BLOCK == 0

    sbw = HEADS * HEAD_DIM
    win = w_in[0]
    o_cq = 3 * sbw
    o_ckv = o_cq + Q_RANK
    o_kr = o_ckv + KV_RANK
    o_gate = o_kr + ROPE_DIM
    wq = win[:, 0:sbw].astype(_BF16)
    wk = win[:, sbw:2 * sbw].astype(_BF16)
    wv = win[:, 2 * sbw:3 * sbw].astype(_BF16)
    wcq = win[:, o_cq:o_ckv].astype(_BF16)
    kr_cols = jnp.zeros((D_MODEL, LANES), _F32).at[:, HEAD_DIM:HEAD_DIM + ROPE_DIM].set(win[:, o_kr:o_gate])
    wckr = jnp.concatenate([win[:, o_ckv:o_kr], kr_cols], axis=1).astype(_BF16)
    wgate = win[:, o_gate:].astype(_BF16)
    wuq = jnp.pad(w_uq[0].reshape(Q_RANK, HEADS, HEAD_DIM + ROPE_DIM),
                  ((0, 0), (0, 0), (0, LANES - HEAD_DIM - ROPE_DIM))).reshape(Q_RANK, HEADS * LANES).astype(_BF16)
    wukv3 = w_ukv[0].reshape(KV_RANK, HEADS, 2 * HEAD_DIM)
    wuk = jnp.pad(wukv3[:, :, :HEAD_DIM], ((0, 0), (0, 0), (0, LANES - HEAD_DIM))).reshape(KV_RANK, HEADS * LANES)
    wuv = wukv3[:, :, HEAD_DIM:].reshape(KV_RANK, HEADS * HEAD_DIM)
    wukv = jnp.concatenate([wuk, wuv], axis=1).astype(_BF16)

    x2 = x.reshape(T, D_MODEL)
    pos = positions.reshape(T, 1).astype(_F32)
    row = lambda v: v.reshape(1, -1).astype(_F32)

    params = pltpu.CompilerParams(dimension_semantics=("parallel",), vmem_limit_bytes=VMEM_LIMIT_BYTES)

    proj_out_widths = [sbw, sbw, sbw, HEADS * LANES, HEADS * LANES, sbw, 2 * D_MODEL]
    qsb, ksb, vsb, qm, km, vm, gates = pl.pallas_call(
        _proj_kernel,
        grid=(T // tm,),
        in_specs=[_row_spec(tm, D_MODEL), _row_spec(tm, 1), _const_spec((8, LANES)), _const_spec((1, D_MODEL)),
                  _const_spec((D_MODEL, sbw)), _const_spec((D_MODEL, sbw)), _const_spec((D_MODEL, sbw)),
                  _const_spec((D_MODEL, Q_RANK)), _const_spec((D_MODEL, KV_RANK + LANES)),
                  _const_spec((D_MODEL, 2 * D_MODEL)), _const_spec((1, 2 * D_MODEL)),
                  _const_spec((1, Q_RANK)), _const_spec((Q_RANK, HEADS * LANES)),
                  _const_spec((1, KV_RANK)), _const_spec((KV_RANK, HEADS * LANES + sbw))],
        out_specs=[_row_spec(tm, w) for w in proj_out_widths],
        out_shape=[jax.ShapeDtypeStruct((T, w), _BF16) for w in proj_out_widths],
        compiler_params=params,
        name="proj",
    )(x2, pos, _rope_table(), row(norm_mix_pre[0]), wq, wk, wv, wcq, wckr, wgate, row(b_gate[0]),
      row(q_norm[0]), wuq, row(kv_norm[0]), wukv)

    cs = CUMSUM_BLOCK
    tri = (lax.broadcasted_iota(jnp.int32, (cs, cs), 0) > lax.broadcasted_iota(jnp.int32, (cs, cs), 1)).astype(_BF16)
    pairs = HEADS // 2
    q_spec = lambda w: pl.BlockSpec((1, tq, w), lambda b, p, i: (b, i, p))
    kv_spec = lambda w: pl.BlockSpec((1, S, w), lambda b, p, i: (b, 0, p))
    osb, om = pl.pallas_call(
        functools.partial(_attn_kernel, tq=tq, tk=tk),
        grid=(B, pairs, S // tq),
        in_specs=[q_spec(PAIR_WIDTH), kv_spec(PAIR_WIDTH), kv_spec(PAIR_WIDTH),
                  q_spec(2 * LANES), kv_spec(2 * LANES), kv_spec(PAIR_WIDTH),
                  pl.BlockSpec((cs, cs), lambda b, p, i: (0, 0))],
        out_specs=[q_spec(PAIR_WIDTH), q_spec(PAIR_WIDTH)],
        out_shape=[jax.ShapeDtypeStruct((B, S, sbw), _BF16)] * 2,
        scratch_shapes=[pltpu.VMEM((2 * tq, tk), _F32), pltpu.VMEM((2 * tq, tk), _BF16),
                        pltpu.VMEM((2 * tq, tk), _F32), pltpu.VMEM((2 * tq, PAIR_WIDTH), _F32),
                        pltpu.VMEM((2 * tq, PAIR_WIDTH), _F32)],
        compiler_params=pltpu.CompilerParams(dimension_semantics=("parallel", "parallel", "arbitrary"),
                                             vmem_limit_bytes=VMEM_LIMIT_BYTES),
        name="attn",
    )(qsb.reshape(B, S, sbw), ksb.reshape(B, S, sbw), vsb.reshape(B, S, sbw),
      qm.reshape(B, S, HEADS * LANES), km.reshape(B, S, HEADS * LANES), vm.reshape(B, S, sbw), tri)

    x1 = pl.pallas_call(
        _mix_kernel,
        grid=(T // tm,),
        in_specs=[_row_spec(tm, D_MODEL), _row_spec(tm, sbw), _row_spec(tm, sbw), _row_spec(tm, 2 * D_MODEL),
                  _const_spec((sbw, D_MODEL)), _const_spec((sbw, D_MODEL)), _const_spec((D_MODEL, D_MODEL)),
                  _const_spec((1, D_MODEL))],
        out_specs=_row_spec(tm, D_MODEL),
        out_shape=jax.ShapeDtypeStruct((T, D_MODEL), _F32),
        compiler_params=params,
        name="mix",
    )(x2, osb.reshape(T, sbw), om.reshape(T, sbw), gates, w_proj_sb[0].astype(_BF16),
      w_proj_mla[0].astype(_BF16), w_out[0].astype(_BF16), row(norm_mix_post[0]))

    out = pl.pallas_call(
        _ffn_kernel,
        grid=(T // tm,),
        in_specs=[_row_spec(tm, D_MODEL), _const_spec((1, D_MODEL)), _const_spec((D_MODEL, 2 * D_FF)),
                  _const_spec((D_FF, D_MODEL)), _const_spec((1, D_MODEL))],
        out_specs=_row_spec(tm, D_MODEL),
        out_shape=jax.ShapeDtypeStruct((T, D_MODEL), _F32),
        compiler_params=params,
        name="ffn",
    )(x1, row(norm_ffn_pre[0]), w_gate_up[0].astype(_BF16), w_down[0].astype(_BF16), row(norm_ffn_post[0]))

    return out.reshape(B, S, D_MODEL)
```

```python
import functools
import math

import numpy as np
import jax
import jax.numpy as jnp
from jax import lax
from jax.experimental import pallas as pl
from jax.experimental.pallas import tpu as pltpu

D_MODEL = 1024
HEADS = 8
HEAD_DIM = 64
ROPE_DIM = 32
Q_RANK = 384
KV_RANK = 256
D_FF = 2816
ROPE_THETA = 10000.0
EPS = 1e-6

LANES = 128
PAIR_WIDTH = 2 * HEAD_DIM
TOKEN_BLOCK = 512
SB_Q_BLOCK = 256
SB_K_BLOCK = 256
MLA_Q_BLOCK = 256
MLA_K_BLOCK = 1024
SB_UNDERFLOW_LOG2 = 160.0
FF_CHUNK = 1408
VMEM_LIMIT_BYTES = 56 * 1024 * 1024

_F32 = jnp.float32
_BF16 = jnp.bfloat16
_NEG = float(np.finfo(np.float32).min)
LOG2E = math.log2(math.e)
_SIGN_BIT = np.uint32(0x80000000)


def _dot(a, b):
    return jnp.dot(a, b, preferred_element_type=_F32)


def _dot_nt(a, b):
    return lax.dot_general(a, b, (((1,), (1,)), ((), ())), preferred_element_type=_F32)


def _rms(x, g):
    return x * lax.rsqrt(jnp.mean(x * x, axis=-1, keepdims=True) + EPS) * g


def _proj_kernel(x_ref, pos_ref, tab_ref, gpre_ref, wq_ref, wk_ref, wv_ref, wcq_ref, wckr_ref,
                 wgate_ref, bgate_ref, qn_ref, wuq_ref, kvn_ref, wukv_ref,
                 qsb_ref, ksb_ref, vsb_ref, qm_ref, km_ref, vm_ref, gates_ref):
    h = _rms(x_ref[...], gpre_ref[...]).astype(_BF16)
    qsb_ref[...] = (_dot(h, wq_ref[...]) * (LOG2E / math.sqrt(HEAD_DIM))).astype(_BF16)
    ksb_ref[...] = _dot(h, wk_ref[...]).astype(_BF16)
    vsb_ref[...] = _dot(h, wv_ref[...]).astype(_BF16)
    gates_ref[...] = jax.nn.sigmoid(_dot(h, wgate_ref[...]) + bgate_ref[...]).astype(_BF16)

    ang = pos_ref[...] * tab_ref[0:1, :]
    cos = jnp.cos(ang)
    sin = jnp.sin(ang)
    s_lo = sin * tab_ref[1:2, :]
    s_hi = sin * tab_ref[2:3, :]

    def rope(v):
        return (v * cos + pltpu.roll(v, LANES - ROPE_DIM // 2, 1) * s_lo
                + pltpu.roll(v, ROPE_DIM // 2, 1) * s_hi)

    cq = _dot(h, wcq_ref[...])
    qm = _dot(_rms(cq, qn_ref[...]).astype(_BF16), wuq_ref[...])
    ckr = _dot(h, wckr_ref[...])
    kv = _dot(_rms(ckr[:, :KV_RANK], kvn_ref[...]).astype(_BF16), wukv_ref[...])
    ones = jnp.ones((kv.shape[0], LANES), _BF16)
    for pr in range(HEADS // 2):
        v_pair = kv[:, HEADS * LANES + pr * PAIR_WIDTH:HEADS * LANES + (pr + 1) * PAIR_WIDTH]
        vm_ref[:, 2 * pr * LANES:(2 * pr + 1) * LANES] = v_pair.astype(_BF16)
        vm_ref[:, (2 * pr + 1) * LANES:(2 * pr + 2) * LANES] = ones
    kr = rope(ckr[:, KV_RANK:])
    q_scale = LOG2E / math.sqrt(HEAD_DIM + ROPE_DIM)
    for hd in range(HEADS):
        sl = slice(hd * LANES, (hd + 1) * LANES)
        qm_ref[:, sl] = (rope(qm[:, sl]) * q_scale).astype(_BF16)
        km_ref[:, sl] = (kv[:, sl] + kr).astype(_BF16)


def _stack_heads(q_pair, split):
    lane = lax.broadcasted_iota(jnp.int32, (1, q_pair.shape[1]), 1)
    zero = jnp.zeros_like(q_pair)
    return jnp.concatenate([jnp.where(lane < split, q_pair, zero), jnp.where(lane >= split, q_pair, zero)], axis=0)


def _unstack_heads(o, tq):
    lane = lax.broadcasted_iota(jnp.int32, (1, PAIR_WIDTH), 1)
    return jnp.where(lane < HEAD_DIM, o[:tq], o[tq:])


def _sb_kernel(q_ref, k_ref, v_ref, tri_ref, o_ref, u_scr, spb_scr, acc_scr, *, tq, tk):
    i = pl.program_id(2)
    q_start = i * tq
    n_full = q_start // tk
    row = lax.broadcasted_iota(jnp.int32, (tq, 1), 0)
    qpos = q_start + jnp.concatenate([row, row], axis=0)
    kiota = lax.broadcasted_iota(jnp.int32, (1, tk), 1)
    tri = tri_ref[...]
    q = _stack_heads(q_ref[0], HEAD_DIM)

    def tile_rows(j):
        return pl.ds(pl.multiple_of(j * tk, tk), tk)

    def score_stage(j, masked):
        z = _dot_nt(q, k_ref[0, tile_rows(j), :])
        neg_abs = lax.bitcast_convert_type(lax.bitcast_convert_type(z, jnp.uint32) | _SIGN_BIT, _F32)
        sp = jnp.maximum(z, 0.0) + jnp.log(1.0 + jnp.exp2(neg_abs)) * LOG2E
        u = z - sp
        if masked:
            valid = (j * tk + kiota) < qpos
            sp = jnp.where(valid, sp, 0.0)
            u = jnp.where(valid, u, _NEG)
        u_scr[...] = u
        spb_scr[...] = sp.astype(_BF16)

    def value_stage(j, carry):
        spb = spb_scr[...]
        later = _dot(spb, tri)
        a = jnp.exp2(u_scr[...] - (later + carry))
        acc_scr[...] += _dot(a.astype(_BF16), v_ref[0, tile_rows(j), :])
        return carry + (later[:, 0:1] + spb[:, 0:1].astype(_F32))

    acc_scr[...] = jnp.zeros_like(acc_scr)
    score_stage(n_full, True)

    def cond(loop_state):
        t, more, _ = loop_state
        return jnp.logical_and(t <= n_full, more > 0)

    def body(loop_state):
        t, _, carry = loop_state
        j = n_full - t
        carry = value_stage(j, carry)
        score_stage(jnp.maximum(j - 1, 0), False)
        more = (jnp.min(carry) < SB_UNDERFLOW_LOG2).astype(jnp.int32)
        return t + 1, more, carry

    lax.while_loop(cond, body, (jnp.int32(0), jnp.int32(1), jnp.zeros((2 * tq, 1), _F32)))
    o_ref[0] = _unstack_heads(acc_scr[...], tq).astype(_BF16)


def _mla_kernel(q_ref, k_ref, v_ref, o_ref, s_scr, tmax_scr, acc_scr, *, tq, tk):
    i = pl.program_id(2)
    q_start = i * tq
    n_full = q_start // tk
    row = lax.broadcasted_iota(jnp.int32, (tq, 1), 0)
    qpos = q_start + jnp.concatenate([row, row], axis=0)
    kiota = lax.broadcasted_iota(jnp.int32, (1, tk), 1)
    q = _stack_heads(q_ref[0], LANES)

    def tile_rows(j):
        return pl.ds(pl.multiple_of(j * tk, tk), tk)

    def score_stage(j, masked):
        s = _dot_nt(q, k_ref[0, tile_rows(j), :])
        if masked:
            s = jnp.where((j * tk + kiota) <= qpos, s, _NEG)
        s_scr[...] = s
        tmax_scr[...] = jnp.max(s, axis=1, keepdims=True)

    def value_stage(j, m):
        m_new = jnp.maximum(m, tmax_scr[...])
        p = jnp.exp2(s_scr[...] - m_new)
        alpha = jnp.exp2(m - m_new)
        acc_scr[...] = alpha * acc_scr[...] + _dot(p.astype(_BF16), v_ref[0, tile_rows(j), :])
        return m_new

    acc_scr[...] = jnp.zeros_like(acc_scr)
    m = jnp.full((2 * tq, 1), _NEG, _F32)
    score_stage(n_full, True)

    def step(t, m):
        j = n_full - t
        m = value_stage(j, m)
        score_stage(j - 1, False)
        return m

    m = lax.fori_loop(0, n_full, step, m)
    value_stage(0, m)
    acc = acc_scr[...]
    o_ref[0] = _unstack_heads(acc[:, :PAIR_WIDTH] / acc[:, PAIR_WIDTH:], tq).astype(_BF16)


def _mix_kernel(x_ref, osb_ref, om_ref, gates_ref, wpsb_ref, wpm_ref, wout_ref, gpost_ref, x1_ref):
    ysb = _dot(osb_ref[...], wpsb_ref[...])
    ym = _dot(om_ref[...], wpm_ref[...])
    g = gates_ref[...].astype(_F32)
    merged = g[:, :D_MODEL] * ysb + g[:, D_MODEL:] * ym
    y = _dot(merged.astype(_BF16), wout_ref[...])
    x1_ref[...] = x_ref[...] + _rms(y, gpost_ref[...])


def _ffn_kernel(x1_ref, gpre_ref, wgu_ref, wdown_ref, gpost_ref, out_ref):
    x1 = x1_ref[...]
    h = _rms(x1, gpre_ref[...]).astype(_BF16)
    f = None
    for c in range(D_FF // FF_CHUNK):
        g = _dot(h, wgu_ref[:, c * FF_CHUNK:(c + 1) * FF_CHUNK])
        u = _dot(h, wgu_ref[:, D_FF + c * FF_CHUNK:D_FF + (c + 1) * FF_CHUNK])
        act = (g * jax.nn.sigmoid(g) * u).astype(_BF16)
        part = _dot(act, wdown_ref[c * FF_CHUNK:(c + 1) * FF_CHUNK, :])
        f = part if f is None else f + part
    out_ref[...] = x1 + _rms(f, gpost_ref[...])


def _const_spec(shape):
    return pl.BlockSpec(shape, lambda *_: (0,) * len(shape))


def _row_spec(tm, width):
    return pl.BlockSpec((tm, width), lambda t: (t, 0))


def _rope_table():
    inv_freq = ROPE_THETA ** (-jnp.arange(0, ROPE_DIM, 2, dtype=_F32) / ROPE_DIM)
    half = ROPE_DIM // 2
    tab = jnp.zeros((8, LANES), _F32)
    tab = tab.at[0, HEAD_DIM:HEAD_DIM + half].set(inv_freq)
    tab = tab.at[0, HEAD_DIM + half:HEAD_DIM + ROPE_DIM].set(inv_freq)
    tab = tab.at[1, HEAD_DIM:HEAD_DIM + half].set(-1.0)
    tab = tab.at[2, HEAD_DIM + half:HEAD_DIM + ROPE_DIM].set(1.0)
    return tab


def kernel(x, positions, norm_mix_pre, norm_mix_post, w_in, b_gate, q_norm, w_uq, kv_norm, w_ukv,
           w_proj_sb, w_proj_mla, w_out, norm_ffn_pre, norm_ffn_post, w_gate_up, w_down):
    B, S, _ = x.shape
    T = B * S
    tm = min(TOKEN_BLOCK, T)
    assert w_in.shape[0] == 1, "single-layer block"
    assert T % tm == 0
    for tq, tk in ((SB_Q_BLOCK, SB_K_BLOCK), (MLA_Q_BLOCK, MLA_K_BLOCK)):
        assert S % tk == 0 and tk % tq == 0

    sbw = HEADS * HEAD_DIM
    win = w_in[0]
    o_cq = 3 * sbw
    o_ckv = o_cq + Q_RANK
    o_kr = o_ckv + KV_RANK
    o_gate = o_kr + ROPE_DIM
    wq = win[:, 0:sbw].astype(_BF16)
    wk = win[:, sbw:2 * sbw].astype(_BF16)
    wv = win[:, 2 * sbw:3 * sbw].astype(_BF16)
    wcq = win[:, o_cq:o_ckv].astype(_BF16)
    kr_cols = jnp.zeros((D_MODEL, LANES), _F32).at[:, HEAD_DIM:HEAD_DIM + ROPE_DIM].set(win[:, o_kr:o_gate])
    wckr = jnp.concatenate([win[:, o_ckv:o_kr], kr_cols], axis=1).astype(_BF16)
    wgate = win[:, o_gate:].astype(_BF16)
    wuq = jnp.pad(w_uq[0].reshape(Q_RANK, HEADS, HEAD_DIM + ROPE_DIM),
                  ((0, 0), (0, 0), (0, LANES - HEAD_DIM - ROPE_DIM))).reshape(Q_RANK, HEADS * LANES).astype(_BF16)
    wukv3 = w_ukv[0].reshape(KV_RANK, HEADS, 2 * HEAD_DIM)
    wuk = jnp.pad(wukv3[:, :, :HEAD_DIM], ((0, 0), (0, 0), (0, LANES - HEAD_DIM))).reshape(KV_RANK, HEADS * LANES)
    wuv = wukv3[:, :, HEAD_DIM:].reshape(KV_RANK, HEADS * HEAD_DIM)
    wukv = jnp.concatenate([wuk, wuv], axis=1).astype(_BF16)

    x2 = x.reshape(T, D_MODEL)
    pos = positions.reshape(T, 1).astype(_F32)
    row = lambda v: v.reshape(1, -1).astype(_F32)

    params = pltpu.CompilerParams(dimension_semantics=("parallel",), vmem_limit_bytes=VMEM_LIMIT_BYTES)

    proj_out_widths = [sbw, sbw, sbw, HEADS * LANES, HEADS * LANES, HEADS * LANES, 2 * D_MODEL]
    qsb, ksb, vsb, qm, km, vm, gates = pl.pallas_call(
        _proj_kernel,
        grid=(T // tm,),
        in_specs=[_row_spec(tm, D_MODEL), _row_spec(tm, 1), _const_spec((8, LANES)), _const_spec((1, D_MODEL)),
                  _const_spec((D_MODEL, sbw)), _const_spec((D_MODEL, sbw)), _const_spec((D_MODEL, sbw)),
                  _const_spec((D_MODEL, Q_RANK)), _const_spec((D_MODEL, KV_RANK + LANES)),
                  _const_spec((D_MODEL, 2 * D_MODEL)), _const_spec((1, 2 * D_MODEL)),
                  _const_spec((1, Q_RANK)), _const_spec((Q_RANK, HEADS * LANES)),
                  _const_spec((1, KV_RANK)), _const_spec((KV_RANK, HEADS * LANES + sbw))],
        out_specs=[_row_spec(tm, w) for w in proj_out_widths],
        out_shape=[jax.ShapeDtypeStruct((T, w), _BF16) for w in proj_out_widths],
        compiler_params=params,
        name="proj",
    )(x2, pos, _rope_table(), row(norm_mix_pre[0]), wq, wk, wv, wcq, wckr, wgate, row(b_gate[0]),
      row(q_norm[0]), wuq, row(kv_norm[0]), wukv)

    pairs = HEADS // 2
    attn_params = pltpu.CompilerParams(dimension_semantics=("parallel", "parallel", "arbitrary"),
                                       vmem_limit_bytes=VMEM_LIMIT_BYTES)

    def q_spec(t, w):
        return pl.BlockSpec((1, t, w), lambda b, p, i: (b, i, p))

    def kv_spec(w):
        return pl.BlockSpec((1, S, w), lambda b, p, i: (b, 0, p))

    tq, tk = SB_Q_BLOCK, SB_K_BLOCK
    tri = (lax.broadcasted_iota(jnp.int32, (tk, tk), 0) > lax.broadcasted_iota(jnp.int32, (tk, tk), 1)).astype(_BF16)
    osb = pl.pallas_call(
        functools.partial(_sb_kernel, tq=tq, tk=tk),
        grid=(B, pairs, S // tq),
        in_specs=[q_spec(tq, PAIR_WIDTH), kv_spec(PAIR_WIDTH), kv_spec(PAIR_WIDTH),
                  pl.BlockSpec((tk, tk), lambda b, p, i: (0, 0))],
        out_specs=q_spec(tq, PAIR_WIDTH),
        out_shape=jax.ShapeDtypeStruct((B, S, sbw), _BF16),
        scratch_shapes=[pltpu.VMEM((2 * tq, tk), _F32), pltpu.VMEM((2 * tq, tk), _BF16),
                        pltpu.VMEM((2 * tq, PAIR_WIDTH), _F32)],
        compiler_params=attn_params,
        name="sb_attn",
    )(qsb.reshape(B, S, sbw), ksb.reshape(B, S, sbw), vsb.reshape(B, S, sbw), tri)

    tq, tk = MLA_Q_BLOCK, MLA_K_BLOCK
    om = pl.pallas_call(
        functools.partial(_mla_kernel, tq=tq, tk=tk),
        grid=(B, pairs, S // tq),
        in_specs=[q_spec(tq, 2 * LANES), kv_spec(2 * LANES), kv_spec(2 * LANES)],
        out_specs=q_spec(tq, PAIR_WIDTH),
        out_shape=jax.ShapeDtypeStruct((B, S, sbw), _BF16),
        scratch_shapes=[pltpu.VMEM((2 * tq, tk), _F32), pltpu.VMEM((2 * tq, 1), _F32),
                        pltpu.VMEM((2 * tq, 2 * LANES), _F32)],
        compiler_params=attn_params,
        name="mla_attn",
    )(qm.reshape(B, S, HEADS * LANES), km.reshape(B, S, HEADS * LANES), vm.reshape(B, S, HEADS * LANES))

    x1 = pl.pallas_call(
        _mix_kernel,
        grid=(T // tm,),
        in_specs=[_row_spec(tm, D_MODEL), _row_spec(tm, sbw), _row_spec(tm, sbw), _row_spec(tm, 2 * D_MODEL),
                  _const_spec((sbw, D_MODEL)), _const_spec((sbw, D_MODEL)), _const_spec((D_MODEL, D_MODEL)),
                  _const_spec((1, D_MODEL))],
        out_specs=_row_spec(tm, D_MODEL),
        out_shape=jax.ShapeDtypeStruct((T, D_MODEL), _F32),
        compiler_params=params,
        name="mix",
    )(x2, osb.reshape(T, sbw), om.reshape(T, sbw), gates, w_proj_sb[0].astype(_BF16),
      w_proj_mla[0].astype(_BF16), w_out[0].astype(_BF16), row(norm_mix_post[0]))

    out = pl.pallas_call(
        _ffn_kernel,
        grid=(T // tm,),
        in_specs=[_row_spec(tm, D_MODEL), _const_spec((1, D_MODEL)), _const_spec((D_MODEL, 2 * D_FF)),
                  _const_spec((D_FF, D_MODEL)), _const_spec((1, D_MODEL))],
        out_specs=_row_spec(tm, D_MODEL),
        out_shape=jax.ShapeDtypeStruct((T, D_MODEL), _F32),
        compiler_params=params,
        name="ffn",
    )(x1, row(norm_ffn_pre[0]), w_gate_up[0].astype(_BF16), w_down[0].astype(_BF16), row(norm_ffn_post[0]))

    return out.reshape(B, S, D_MODEL)
```

```python
import functools
import math

import numpy as np
import jax
import jax.numpy as jnp
from jax import lax
from jax.experimental import pallas as pl
from jax.experimental.pallas import tpu as pltpu

D_MODEL = 1024
HEADS = 8
HEAD_DIM = 64
ROPE_DIM = 32
Q_RANK = 384
KV_RANK = 256
D_FF = 2816
ROPE_THETA = 10000.0
EPS = 1e-6

LANES = 128
PAIR_WIDTH = 2 * HEAD_DIM
TOKEN_BLOCK = 512
SB_Q_BLOCK = 256
SB_K_BLOCK = 256
SB_CHAINS = 2
MLA_Q_BLOCK = 512
MLA_K_BLOCK = 1024
SB_UNDERFLOW_LOG2 = 160.0
FF_CHUNK = 1408
VMEM_LIMIT_BYTES = 56 * 1024 * 1024

_F32 = jnp.float32
_BF16 = jnp.bfloat16
_NEG = float(np.finfo(np.float32).min)
LOG2E = math.log2(math.e)
_SIGN_BIT = np.uint32(0x80000000)


def _dot(a, b):
    return jnp.dot(a, b, preferred_element_type=_F32)


def _dot_nt(a, b):
    return lax.dot_general(a, b, (((1,), (1,)), ((), ())), preferred_element_type=_F32)


def _rms(x, g):
    return x * lax.rsqrt(jnp.mean(x * x, axis=-1, keepdims=True) + EPS) * g


def _proj_kernel(x_ref, pos_ref, tab_ref, gpre_ref, wq_ref, wk_ref, wv_ref, wcq_ref, wckr_ref,
                 wgate_ref, bgate_ref, qn_ref, wuq_ref, kvn_ref, wukv_ref,
                 qsb_ref, ksb_ref, vsb_ref, qm_ref, km_ref, vm_ref, gates_ref):
    h = _rms(x_ref[...], gpre_ref[...]).astype(_BF16)
    qsb_ref[...] = (_dot(h, wq_ref[...]) * (LOG2E / math.sqrt(HEAD_DIM))).astype(_BF16)
    ksb_ref[...] = _dot(h, wk_ref[...]).astype(_BF16)
    vsb_ref[...] = _dot(h, wv_ref[...]).astype(_BF16)
    gates_ref[...] = jax.nn.sigmoid(_dot(h, wgate_ref[...]) + bgate_ref[...]).astype(_BF16)

    ang = pos_ref[...] * tab_ref[0:1, :]
    cos = jnp.cos(ang)
    sin = jnp.sin(ang)
    s_lo = sin * tab_ref[1:2, :]
    s_hi = sin * tab_ref[2:3, :]

    def rope(v):
        return (v * cos + pltpu.roll(v, LANES - ROPE_DIM // 2, 1) * s_lo
                + pltpu.roll(v, ROPE_DIM // 2, 1) * s_hi)

    cq = _dot(h, wcq_ref[...])
    qm = _dot(_rms(cq, qn_ref[...]).astype(_BF16), wuq_ref[...])
    ckr = _dot(h, wckr_ref[...])
    kv = _dot(_rms(ckr[:, :KV_RANK], kvn_ref[...]).astype(_BF16), wukv_ref[...])
    ones = jnp.ones((kv.shape[0], LANES), _BF16)
    for pr in range(HEADS // 2):
        v_pair = kv[:, HEADS * LANES + pr * PAIR_WIDTH:HEADS * LANES + (pr + 1) * PAIR_WIDTH]
        vm_ref[:, 2 * pr * LANES:(2 * pr + 1) * LANES] = v_pair.astype(_BF16)
        vm_ref[:, (2 * pr + 1) * LANES:(2 * pr + 2) * LANES] = ones
    kr = rope(ckr[:, KV_RANK:])
    q_scale = LOG2E / math.sqrt(HEAD_DIM + ROPE_DIM)
    for hd in range(HEADS):
        sl = slice(hd * LANES, (hd + 1) * LANES)
        qm_ref[:, sl] = (rope(qm[:, sl]) * q_scale).astype(_BF16)
        km_ref[:, sl] = (kv[:, sl] + kr).astype(_BF16)


def _stack_heads(q_pair, split):
    lane = lax.broadcasted_iota(jnp.int32, (1, q_pair.shape[1]), 1)
    zero = jnp.zeros_like(q_pair)
    return jnp.concatenate([jnp.where(lane < split, q_pair, zero), jnp.where(lane >= split, q_pair, zero)], axis=0)


def _unstack_heads(o, tq):
    lane = lax.broadcasted_iota(jnp.int32, (1, PAIR_WIDTH), 1)
    return jnp.where(lane < HEAD_DIM, o[:tq], o[tq:])


def _sb_kernel(q_ref, k_ref, v_ref, tri_ref, o_ref, u_scr, spb_scr, acc_scr, *, tq, tk):
    g = pl.program_id(2)
    kiota = lax.broadcasted_iota(jnp.int32, (1, tk), 1)
    row = lax.broadcasted_iota(jnp.int32, (tq, 1), 0)
    tri = tri_ref[...]
    n_diag = [(g * SB_CHAINS + c) * tq // tk for c in range(SB_CHAINS)]
    qpos = [(g * SB_CHAINS + c) * tq + jnp.concatenate([row, row], axis=0) for c in range(SB_CHAINS)]
    q = [_stack_heads(q_ref[0, c * tq:(c + 1) * tq, :], HEAD_DIM) for c in range(SB_CHAINS)]

    def tile_rows(j):
        return pl.ds(pl.multiple_of(j * tk, tk), tk)

    def score_stage(c, j, masked):
        z = _dot_nt(q[c], k_ref[0, tile_rows(j), :])
        neg_abs = lax.bitcast_convert_type(lax.bitcast_convert_type(z, jnp.uint32) | _SIGN_BIT, _F32)
        sp = jnp.maximum(z, 0.0) + jnp.log(1.0 + jnp.exp2(neg_abs)) * LOG2E
        u = z - sp
        if masked:
            valid = (j * tk + kiota) < qpos[c]
            sp = jnp.where(valid, sp, 0.0)
            u = jnp.where(valid, u, _NEG)
        u_scr[c] = u
        spb_scr[c] = sp.astype(_BF16)

    def value_stage(c, j, carry):
        spb = spb_scr[c]
        later = _dot(spb, tri)
        a = jnp.exp2(u_scr[c] - (later + carry))
        acc_scr[c] += _dot(a.astype(_BF16), v_ref[0, tile_rows(j), :])
        return carry + (later[:, 0:1] + spb[:, 0:1].astype(_F32))

    acc_scr[...] = jnp.zeros_like(acc_scr)
    for c in range(SB_CHAINS):
        score_stage(c, n_diag[c], True)

    def cond(loop_state):
        return loop_state[1] > 0

    def body(loop_state):
        t, _, carries = loop_state
        new_carries = []
        for c in range(SB_CHAINS):
            j = jnp.maximum(n_diag[c] - t, 0)
            carry = value_stage(c, j, carries[c])
            score_stage(c, jnp.maximum(j - 1, 0), False)
            new_carries.append(jnp.where(n_diag[c] - t <= 0, jnp.inf, carry))
        least = functools.reduce(jnp.minimum, [jnp.min(cr) for cr in new_carries])
        return t + 1, (least < SB_UNDERFLOW_LOG2).astype(jnp.int32), tuple(new_carries)

    carry0 = tuple(jnp.zeros((2 * tq, 1), _F32) for _ in range(SB_CHAINS))
    lax.while_loop(cond, body, (jnp.int32(0), jnp.int32(1), carry0))
    for c in range(SB_CHAINS):
        o_ref[0, c * tq:(c + 1) * tq, :] = _unstack_heads(acc_scr[c], tq).astype(_BF16)


def _mla_kernel(q_ref, k_ref, v_ref, o_ref, s_scr, tmax_scr, acc_scr, *, tq, tk):
    i = pl.program_id(2)
    q_start = i * tq
    n_full = q_start // tk
    row = lax.broadcasted_iota(jnp.int32, (tq, 1), 0)
    qpos = q_start + jnp.concatenate([row, row], axis=0)
    kiota = lax.broadcasted_iota(jnp.int32, (1, tk), 1)
    q = _stack_heads(q_ref[0], LANES)

    def tile_rows(j):
        return pl.ds(pl.multiple_of(j * tk, tk), tk)

    def score_stage(j, masked):
        s = _dot_nt(q, k_ref[0, tile_rows(j), :])
        if masked:
            s = jnp.where((j * tk + kiota) <= qpos, s, _NEG)
        s_scr[...] = s
        tmax_scr[...] = jnp.max(s, axis=1, keepdims=True)

    def value_stage(j, m):
        m_new = jnp.maximum(m, tmax_scr[...])
        p = jnp.exp2(s_scr[...] - m_new)
        alpha = jnp.exp2(m - m_new)
        acc_scr[...] = alpha * acc_scr[...] + _dot(p.astype(_BF16), v_ref[0, tile_rows(j), :])
        return m_new

    acc_scr[...] = jnp.zeros_like(acc_scr)
    m = jnp.full((2 * tq, 1), _NEG, _F32)
    score_stage(n_full, True)

    def step(t, m):
        j = n_full - t
        m = value_stage(j, m)
        score_stage(j - 1, False)
        return m

    m = lax.fori_loop(0, n_full, step, m)
    value_stage(0, m)
    acc = acc_scr[...]
    o_ref[0] = _unstack_heads(acc[:, :PAIR_WIDTH] / acc[:, PAIR_WIDTH:], tq).astype(_BF16)


def _tail_kernel(x_ref, osb_ref, om_ref, gates_ref, wpsb_ref, wpm_ref, wout_ref, gmix_ref,
                 gpre_ref, wgu_ref, wdown_ref, gpost_ref, out_ref):
    ysb = _dot(osb_ref[...], wpsb_ref[...])
    ym = _dot(om_ref[...], wpm_ref[...])
    g = gates_ref[...].astype(_F32)
    merged = g[:, :D_MODEL] * ysb + g[:, D_MODEL:] * ym
    y = _dot(merged.astype(_BF16), wout_ref[...])
    x1 = x_ref[...] + _rms(y, gmix_ref[...])

    h = _rms(x1, gpre_ref[...]).astype(_BF16)
    f = None
    for c in range(D_FF // FF_CHUNK):
        g = _dot(h, wgu_ref[:, c * FF_CHUNK:(c + 1) * FF_CHUNK])
        u = _dot(h, wgu_ref[:, D_FF + c * FF_CHUNK:D_FF + (c + 1) * FF_CHUNK])
        act = (g * jax.nn.sigmoid(g) * u).astype(_BF16)
        part = _dot(act, wdown_ref[c * FF_CHUNK:(c + 1) * FF_CHUNK, :])
        f = part if f is None else f + part
    out_ref[...] = x1 + _rms(f, gpost_ref[...])


def _const_spec(shape):
    return pl.BlockSpec(shape, lambda *_: (0,) * len(shape))


def _weight_spec(shape):
    return pl.BlockSpec(shape, lambda *_: (0,) * len(shape), pipeline_mode=pl.Buffered(1))


def _row_spec(tm, width):
    return pl.BlockSpec((tm, width), lambda t: (t, 0))


def _rope_table():
    inv_freq = ROPE_THETA ** (-jnp.arange(0, ROPE_DIM, 2, dtype=_F32) / ROPE_DIM)
    half = ROPE_DIM // 2
    tab = jnp.zeros((8, LANES), _F32)
    tab = tab.at[0, HEAD_DIM:HEAD_DIM + half].set(inv_freq)
    tab = tab.at[0, HEAD_DIM + half:HEAD_DIM + ROPE_DIM].set(inv_freq)
    tab = tab.at[1, HEAD_DIM:HEAD_DIM + half].set(-1.0)
    tab = tab.at[2, HEAD_DIM + half:HEAD_DIM + ROPE_DIM].set(1.0)
    return tab


def kernel(x, positions, norm_mix_pre, norm_mix_post, w_in, b_gate, q_norm, w_uq, kv_norm, w_ukv,
           w_proj_sb, w_proj_mla, w_out, norm_ffn_pre, norm_ffn_post, w_gate_up, w_down):
    B, S, _ = x.shape
    T = B * S
    tm = min(TOKEN_BLOCK, T)
    assert w_in.shape[0] == 1, "single-layer block"
    assert T % tm == 0
    for tq, tk in ((SB_Q_BLOCK, SB_K_BLOCK), (MLA_Q_BLOCK, MLA_K_BLOCK)):
        assert S % tk == 0 and tk % tq == 0
    assert S % (SB_CHAINS * SB_Q_BLOCK) == 0

    sbw = HEADS * HEAD_DIM
    win = w_in[0]
    o_cq = 3 * sbw
    o_ckv = o_cq + Q_RANK
    o_kr = o_ckv + KV_RANK
    o_gate = o_kr + ROPE_DIM
    wq = win[:, 0:sbw].astype(_BF16)
    wk = win[:, sbw:2 * sbw].astype(_BF16)
    wv = win[:, 2 * sbw:3 * sbw].astype(_BF16)
    wcq = win[:, o_cq:o_ckv].astype(_BF16)
    kr_cols = jnp.zeros((D_MODEL, LANES), _F32).at[:, HEAD_DIM:HEAD_DIM + ROPE_DIM].set(win[:, o_kr:o_gate])
    wckr = jnp.concatenate([win[:, o_ckv:o_kr], kr_cols], axis=1).astype(_BF16)
    wgate = win[:, o_gate:].astype(_BF16)
    wuq = jnp.pad(w_uq[0].reshape(Q_RANK, HEADS, HEAD_DIM + ROPE_DIM),
                  ((0, 0), (0, 0), (0, LANES - HEAD_DIM - ROPE_DIM))).reshape(Q_RANK, HEADS * LANES).astype(_BF16)
    wukv3 = w_ukv[0].reshape(KV_RANK, HEADS, 2 * HEAD_DIM)
    wuk = jnp.pad(wukv3[:, :, :HEAD_DIM], ((0, 0), (0, 0), (0, LANES - HEAD_DIM))).reshape(KV_RANK, HEADS * LANES)
    wuv = wukv3[:, :, HEAD_DIM:].reshape(KV_RANK, HEADS * HEAD_DIM)
    wukv = jnp.concatenate([wuk, wuv], axis=1).astype(_BF16)

    x2 = x.reshape(T, D_MODEL)
    pos = positions.reshape(T, 1).astype(_F32)
    row = lambda v: v.reshape(1, -1).astype(_F32)

    params = pltpu.CompilerParams(dimension_semantics=("parallel",), vmem_limit_bytes=VMEM_LIMIT_BYTES)

    proj_out_widths = [sbw, sbw, sbw, HEADS * LANES, HEADS * LANES, HEADS * LANES, 2 * D_MODEL]
    qsb, ksb, vsb, qm, km, vm, gates = pl.pallas_call(
        _proj_kernel,
        grid=(T // tm,),
        in_specs=[_row_spec(tm, D_MODEL), _row_spec(tm, 1), _const_spec((8, LANES)), _const_spec((1, D_MODEL)),
                  _const_spec((D_MODEL, sbw)), _const_spec((D_MODEL, sbw)), _const_spec((D_MODEL, sbw)),
                  _const_spec((D_MODEL, Q_RANK)), _const_spec((D_MODEL, KV_RANK + LANES)),
                  _const_spec((D_MODEL, 2 * D_MODEL)), _const_spec((1, 2 * D_MODEL)),
                  _const_spec((1, Q_RANK)), _const_spec((Q_RANK, HEADS * LANES)),
                  _const_spec((1, KV_RANK)), _const_spec((KV_RANK, HEADS * LANES + sbw))],
        out_specs=[_row_spec(tm, w) for w in proj_out_widths],
        out_shape=[jax.ShapeDtypeStruct((T, w), _BF16) for w in proj_out_widths],
        compiler_params=params,
        name="proj",
    )(x2, pos, _rope_table(), row(norm_mix_pre[0]), wq, wk, wv, wcq, wckr, wgate, row(b_gate[0]),
      row(q_norm[0]), wuq, row(kv_norm[0]), wukv)

    pairs = HEADS // 2
    attn_params = pltpu.CompilerParams(dimension_semantics=("parallel", "parallel", "arbitrary"),
                                       vmem_limit_bytes=VMEM_LIMIT_BYTES)

    def q_spec(t, w):
        return pl.BlockSpec((1, t, w), lambda b, p, i: (b, i, p))

    def kv_spec(w):
        return pl.BlockSpec((1, S, w), lambda b, p, i: (b, 0, p))

    tq, tk = SB_Q_BLOCK, SB_K_BLOCK
    tri = (lax.broadcasted_iota(jnp.int32, (tk, tk), 0) > lax.broadcasted_iota(jnp.int32, (tk, tk), 1)).astype(_BF16)
    osb = pl.pallas_call(
        functools.partial(_sb_kernel, tq=tq, tk=tk),
        grid=(B, pairs, S // (SB_CHAINS * tq)),
        in_specs=[q_spec(SB_CHAINS * tq, PAIR_WIDTH), kv_spec(PAIR_WIDTH), kv_spec(PAIR_WIDTH),
                  pl.BlockSpec((tk, tk), lambda b, p, i: (0, 0))],
        out_specs=q_spec(SB_CHAINS * tq, PAIR_WIDTH),
        out_shape=jax.ShapeDtypeStruct((B, S, sbw), _BF16),
        scratch_shapes=[pltpu.VMEM((SB_CHAINS, 2 * tq, tk), _F32), pltpu.VMEM((SB_CHAINS, 2 * tq, tk), _BF16),
                        pltpu.VMEM((SB_CHAINS, 2 * tq, PAIR_WIDTH), _F32)],
        compiler_params=attn_params,
        name="sb_attn",
    )(qsb.reshape(B, S, sbw), ksb.reshape(B, S, sbw), vsb.reshape(B, S, sbw), tri)

    tq, tk = MLA_Q_BLOCK, MLA_K_BLOCK
    om = pl.pallas_call(
        functools.partial(_mla_kernel, tq=tq, tk=tk),
        grid=(B, pairs, S // tq),
        in_specs=[q_spec(tq, 2 * LANES), kv_spec(2 * LANES), kv_spec(2 * LANES)],
        out_specs=q_spec(tq, PAIR_WIDTH),
        out_shape=jax.ShapeDtypeStruct((B, S, sbw), _BF16),
        scratch_shapes=[pltpu.VMEM((2 * tq, tk), _F32), pltpu.VMEM((2 * tq, 1), _F32),
                        pltpu.VMEM((2 * tq, 2 * LANES), _F32)],
        compiler_params=attn_params,
        name="mla_attn",
    )(qm.reshape(B, S, HEADS * LANES), km.reshape(B, S, HEADS * LANES), vm.reshape(B, S, HEADS * LANES))

    out = pl.pallas_call(
        _tail_kernel,
        grid=(T // tm,),
        in_specs=[_row_spec(tm, D_MODEL), _row_spec(tm, sbw), _row_spec(tm, sbw), _row_spec(tm, 2 * D_MODEL),
                  _weight_spec((sbw, D_MODEL)), _weight_spec((sbw, D_MODEL)), _weight_spec((D_MODEL, D_MODEL)),
                  _const_spec((1, D_MODEL)), _const_spec((1, D_MODEL)), _weight_spec((D_MODEL, 2 * D_FF)),
                  _weight_spec((D_FF, D_MODEL)), _const_spec((1, D_MODEL))],
        out_specs=_row_spec(tm, D_MODEL),
        out_shape=jax.ShapeDtypeStruct((T, D_MODEL), _F32),
        compiler_params=params,
        name="tail",
    )(x2, osb.reshape(T, sbw), om.reshape(T, sbw), gates, w_proj_sb[0].astype(_BF16),
      w_proj_mla[0].astype(_BF16), w_out[0].astype(_BF16), row(norm_mix_post[0]),
      row(norm_ffn_pre[0]), w_gate_up[0].astype(_BF16), w_down[0].astype(_BF16), row(norm_ffn_post[0]))

    return out.reshape(B, S, D_MODEL)
```

```python
import functools
import math

import numpy as np
import jax
import jax.numpy as jnp
from jax import lax
from jax.experimental import pallas as pl
from jax.experimental.pallas import tpu as pltpu

D_MODEL = 1024
HEADS = 8
HEAD_DIM = 64
ROPE_DIM = 32
Q_RANK = 384
KV_RANK = 256
D_FF = 2816
ROPE_THETA = 10000.0
EPS = 1e-6

LANES = 128
PAIR_WIDTH = 2 * HEAD_DIM
TOKEN_BLOCK = 512
SB_Q_BLOCK = 256
SB_K_BLOCK = 256
MLA_Q_BLOCK = 512
MLA_K_BLOCK = 1024
SB_UNDERFLOW_LOG2 = 160.0
FF_CHUNK = 1408
VMEM_LIMIT_BYTES = 56 * 1024 * 1024

_F32 = jnp.float32
_BF16 = jnp.bfloat16
_NEG = float(np.finfo(np.float32).min)
LOG2E = math.log2(math.e)
_SIGN_BIT = np.uint32(0x80000000)


def _dot(a, b):
    return jnp.dot(a, b, preferred_element_type=_F32)


def _dot_nt(a, b):
    return lax.dot_general(a, b, (((1,), (1,)), ((), ())), preferred_element_type=_F32)


def _rms(x, g):
    return x * lax.rsqrt(jnp.mean(x * x, axis=-1, keepdims=True) + EPS) * g


def _proj_kernel(x_ref, pos_ref, tab_ref, gpre_ref, wq_ref, wk_ref, wv_ref, wcq_ref, wckr_ref,
                 wgate_ref, bgate_ref, qn_ref, wuq_ref, kvn_ref, wukv_ref,
                 qsb_ref, ksb_ref, vsb_ref, qm_ref, km_ref, vm_ref, gates_ref):
    h = _rms(x_ref[...], gpre_ref[...]).astype(_BF16)
    qsb_ref[...] = (_dot(h, wq_ref[...]) * (LOG2E / math.sqrt(HEAD_DIM))).astype(_BF16)
    ksb_ref[...] = _dot(h, wk_ref[...]).astype(_BF16)
    vsb_ref[...] = _dot(h, wv_ref[...]).astype(_BF16)
    gates_ref[...] = jax.nn.sigmoid(_dot(h, wgate_ref[...]) + bgate_ref[...]).astype(_BF16)

    ang = pos_ref[...] * tab_ref[0:1, :]
    cos = jnp.cos(ang)
    sin = jnp.sin(ang)
    s_lo = sin * tab_ref[1:2, :]
    s_hi = sin * tab_ref[2:3, :]

    def rope(v):
        return (v * cos + pltpu.roll(v, LANES - ROPE_DIM // 2, 1) * s_lo
                + pltpu.roll(v, ROPE_DIM // 2, 1) * s_hi)

    cq = _dot(h, wcq_ref[...])
    qm = _dot(_rms(cq, qn_ref[...]).astype(_BF16), wuq_ref[...])
    ckr = _dot(h, wckr_ref[...])
    kv = _dot(_rms(ckr[:, :KV_RANK], kvn_ref[...]).astype(_BF16), wukv_ref[...])
    ones = jnp.ones((kv.shape[0], LANES), _BF16)
    for pr in range(HEADS // 2):
        v_pair = kv[:, HEADS * LANES + pr * PAIR_WIDTH:HEADS * LANES + (pr + 1) * PAIR_WIDTH]
        vm_ref[:, 2 * pr * LANES:(2 * pr + 1) * LANES] = v_pair.astype(_BF16)
        vm_ref[:, (2 * pr + 1) * LANES:(2 * pr + 2) * LANES] = ones
    kr = rope(ckr[:, KV_RANK:])
    q_scale = LOG2E / math.sqrt(HEAD_DIM + ROPE_DIM)
    for hd in range(HEADS):
        sl = slice(hd * LANES, (hd + 1) * LANES)
        qm_ref[:, sl] = (rope(qm[:, sl]) * q_scale).astype(_BF16)
        km_ref[:, sl] = (kv[:, sl] + kr).astype(_BF16)


def _stack_heads(q_pair, split):
    lane = lax.broadcasted_iota(jnp.int32, (1, q_pair.shape[1]), 1)
    zero = jnp.zeros_like(q_pair)
    return jnp.concatenate([jnp.where(lane < split, q_pair, zero), jnp.where(lane >= split, q_pair, zero)], axis=0)


def _unstack_heads(o, tq):
    lane = lax.broadcasted_iota(jnp.int32, (1, PAIR_WIDTH), 1)
    return jnp.where(lane < HEAD_DIM, o[:tq], o[tq:])


def _attn_kernel(qs_ref, ks_ref, vs_ref, tri_ref, qm_ref, km_ref, vm_ref, os_ref, om_ref,
                 u_scr, spb_scr, tot_scr, acc_s_scr, s_scr, tmax_scr, acc_m_scr, *, tq, tk_s, tk_m):
    g = pl.program_id(2)
    tq_s = SB_Q_BLOCK
    chains = tq // tq_s
    tri = tri_ref[...]
    row_s = lax.broadcasted_iota(jnp.int32, (tq_s, 1), 0)
    kiota_s = lax.broadcasted_iota(jnp.int32, (1, tk_s), 1)
    n_diag = [(g * tq + c * tq_s) // tk_s for c in range(chains)]
    qpos_s = [g * tq + c * tq_s + jnp.concatenate([row_s, row_s], axis=0) for c in range(chains)]
    q_s = [_stack_heads(qs_ref[0, c * tq_s:(c + 1) * tq_s, :], HEAD_DIM) for c in range(chains)]

    def rows_s(j):
        return pl.ds(pl.multiple_of(j * tk_s, tk_s), tk_s)

    def sb_score(c, j, masked):
        z = _dot_nt(q_s[c], ks_ref[0, rows_s(j), :])
        neg_abs = lax.bitcast_convert_type(lax.bitcast_convert_type(z, jnp.uint32) | _SIGN_BIT, _F32)
        sp = jnp.maximum(z, 0.0) + jnp.log(1.0 + jnp.exp2(neg_abs)) * LOG2E
        u = z - sp
        if masked:
            valid = (j * tk_s + kiota_s) < qpos_s[c]
            sp = jnp.where(valid, sp, 0.0)
            u = jnp.where(valid, u, _NEG)
        u_scr[c] = u
        spb_scr[c] = sp.astype(_BF16)
        tot_scr[c] = jnp.sum(sp, axis=1, keepdims=True)

    def sb_value(c, j, carry):
        spb = spb_scr[c]
        later = _dot(spb, tri)
        a = jnp.exp2(u_scr[c] - (later + carry))
        acc_s_scr[c] += _dot(a.astype(_BF16), vs_ref[0, rows_s(j), :])
        return carry + (later[:, 0:1] + spb[:, 0:1].astype(_F32))

    def sb_more(t, carries):
        flags = [jnp.logical_and(n_diag[c] - t >= 1, jnp.min(carries[c] + tot_scr[c]) < SB_UNDERFLOW_LOG2)
                 for c in range(chains)]
        return functools.reduce(jnp.logical_or, flags).astype(jnp.int32)

    def sb_step(t, carries):
        new_carries = []
        for c in range(chains):
            j = jnp.maximum(n_diag[c] - t, 0)
            carry = sb_value(c, j, carries[c])
            sb_score(c, jnp.maximum(j - 1, 0), False)
            new_carries.append(jnp.where(n_diag[c] - t <= 0, jnp.inf, carry))
        new_carries = tuple(new_carries)
        return sb_more(t + 1, new_carries), new_carries

    n_m = (g * tq) // tk_m
    row_m = lax.broadcasted_iota(jnp.int32, (tq, 1), 0)
    qpos_m = g * tq + jnp.concatenate([row_m, row_m], axis=0)
    kiota_m = lax.broadcasted_iota(jnp.int32, (1, tk_m), 1)
    q_m = _stack_heads(qm_ref[0], LANES)

    def rows_m(j):
        return pl.ds(pl.multiple_of(j * tk_m, tk_m), tk_m)

    def mla_score(j, masked):
        s = _dot_nt(q_m, km_ref[0, rows_m(j), :])
        if masked:
            s = jnp.where((j * tk_m + kiota_m) <= qpos_m, s, _NEG)
        s_scr[...] = s
        tmax_scr[...] = jnp.max(s, axis=1, keepdims=True)

    def mla_value(j, m):
        m_new = jnp.maximum(m, tmax_scr[...])
        p = jnp.exp2(s_scr[...] - m_new)
        alpha = jnp.exp2(m - m_new)
        acc_m_scr[...] = alpha * acc_m_scr[...] + _dot(p.astype(_BF16), vm_ref[0, rows_m(j), :])
        return m_new

    def mla_step(t, m):
        j = n_m - t
        m = mla_value(j, m)
        mla_score(j - 1, False)
        return m

    acc_s_scr[...] = jnp.zeros_like(acc_s_scr)
    acc_m_scr[...] = jnp.zeros_like(acc_m_scr)
    mla_score(n_m, True)
    for c in range(chains):
        sb_score(c, n_diag[c], True)

    def both_cond(st):
        t, more, _, _ = st
        return jnp.logical_and(t < n_m, more > 0)

    def both_body(st):
        t, _, carries, m = st
        m = mla_step(t, m)
        more, carries = sb_step(t, carries)
        return t + 1, more, carries, m

    carries = tuple(jnp.zeros((2 * tq_s, 1), _F32) for _ in range(chains))
    m = jnp.full((2 * tq, 1), _NEG, _F32)
    t0 = jnp.int32(0)
    t, more, carries, m = lax.while_loop(both_cond, both_body, (t0, sb_more(t0, carries), carries, m))

    m = lax.fori_loop(t, n_m, mla_step, m)

    def sb_body(st):
        t, _, carries = st
        more, carries = sb_step(t, carries)
        return t + 1, more, carries

    t, _, carries = lax.while_loop(lambda st: st[1] > 0, sb_body, (t, more, carries))

    mla_value(0, m)
    for c in range(chains):
        sb_value(c, jnp.maximum(n_diag[c] - t, 0), carries[c])

    for c in range(chains):
        os_ref[0, c * tq_s:(c + 1) * tq_s, :] = _unstack_heads(acc_s_scr[c], tq_s).astype(_BF16)
    acc = acc_m_scr[...]
    om_ref[0] = _unstack_heads(acc[:, :PAIR_WIDTH] / acc[:, PAIR_WIDTH:], tq).astype(_BF16)


def _tail_kernel(x_ref, osb_ref, om_ref, gates_ref, wpsb_ref, wpm_ref, wout_ref, gmix_ref,
                 gpre_ref, wgu_ref, wdown_ref, gpost_ref, out_ref):
    ysb = _dot(osb_ref[...], wpsb_ref[...])
    ym = _dot(om_ref[...], wpm_ref[...])
    g = gates_ref[...].astype(_F32)
    merged = g[:, :D_MODEL] * ysb + g[:, D_MODEL:] * ym
    y = _dot(merged.astype(_BF16), wout_ref[...])
    x1 = x_ref[...] + _rms(y, gmix_ref[...])

    h = _rms(x1, gpre_ref[...]).astype(_BF16)
    f = None
    for c in range(D_FF // FF_CHUNK):
        g = _dot(h, wgu_ref[:, c * FF_CHUNK:(c + 1) * FF_CHUNK])
        u = _dot(h, wgu_ref[:, D_FF + c * FF_CHUNK:D_FF + (c + 1) * FF_CHUNK])
        act = (g * jax.nn.sigmoid(g) * u).astype(_BF16)
        part = _dot(act, wdown_ref[c * FF_CHUNK:(c + 1) * FF_CHUNK, :])
        f = part if f is None else f + part
    out_ref[...] = x1 + _rms(f, gpost_ref[...])


def _const_spec(shape):
    return pl.BlockSpec(shape, lambda *_: (0,) * len(shape))


def _weight_spec(shape):
    return pl.BlockSpec(shape, lambda *_: (0,) * len(shape), pipeline_mode=pl.Buffered(1))


def _row_spec(tm, width):
    return pl.BlockSpec((tm, width), lambda t: (t, 0))


def _rope_table():
    inv_freq = ROPE_THETA ** (-jnp.arange(0, ROPE_DIM, 2, dtype=_F32) / ROPE_DIM)
    half = ROPE_DIM // 2
    tab = jnp.zeros((8, LANES), _F32)
    tab = tab.at[0, HEAD_DIM:HEAD_DIM + half].set(inv_freq)
    tab = tab.at[0, HEAD_DIM + half:HEAD_DIM + ROPE_DIM].set(inv_freq)
    tab = tab.at[1, HEAD_DIM:HEAD_DIM + half].set(-1.0)
    tab = tab.at[2, HEAD_DIM + half:HEAD_DIM + ROPE_DIM].set(1.0)
    return tab


def kernel(x, positions, norm_mix_pre, norm_mix_post, w_in, b_gate, q_norm, w_uq, kv_norm, w_ukv,
           w_proj_sb, w_proj_mla, w_out, norm_ffn_pre, norm_ffn_post, w_gate_up, w_down):
    B, S, _ = x.shape
    T = B * S
    tm = min(TOKEN_BLOCK, T)
    assert w_in.shape[0] == 1, "single-layer block"
    assert T % tm == 0
    for tq, tk in ((SB_Q_BLOCK, SB_K_BLOCK), (MLA_Q_BLOCK, MLA_K_BLOCK)):
        assert S % tk == 0 and tk % tq == 0
    assert MLA_Q_BLOCK % SB_Q_BLOCK == 0 and S % MLA_Q_BLOCK == 0

    sbw = HEADS * HEAD_DIM
    win = w_in[0]
    o_cq = 3 * sbw
    o_ckv = o_cq + Q_RANK
    o_kr = o_ckv + KV_RANK
    o_gate = o_kr + ROPE_DIM
    wq = win[:, 0:sbw].astype(_BF16)
    wk = win[:, sbw:2 * sbw].astype(_BF16)
    wv = win[:, 2 * sbw:3 * sbw].astype(_BF16)
    wcq = win[:, o_cq:o_ckv].astype(_BF16)
    kr_cols = jnp.zeros((D_MODEL, LANES), _F32).at[:, HEAD_DIM:HEAD_DIM + ROPE_DIM].set(win[:, o_kr:o_gate])
    wckr = jnp.concatenate([win[:, o_ckv:o_kr], kr_cols], axis=1).astype(_BF16)
    wgate = win[:, o_gate:].astype(_BF16)
    wuq = jnp.pad(w_uq[0].reshape(Q_RANK, HEADS, HEAD_DIM + ROPE_DIM),
                  ((0, 0), (0, 0), (0, LANES - HEAD_DIM - ROPE_DIM))).reshape(Q_RANK, HEADS * LANES).astype(_BF16)
    wukv3 = w_ukv[0].reshape(KV_RANK, HEADS, 2 * HEAD_DIM)
    wuk = jnp.pad(wukv3[:, :, :HEAD_DIM], ((0, 0), (0, 0), (0, LANES - HEAD_DIM))).reshape(KV_RANK, HEADS * LANES)
    wuv = wukv3[:, :, HEAD_DIM:].reshape(KV_RANK, HEADS * HEAD_DIM)
    wukv = jnp.concatenate([wuk, wuv], axis=1).astype(_BF16)

    x2 = x.reshape(T, D_MODEL)
    pos = positions.reshape(T, 1).astype(_F32)
    row = lambda v: v.reshape(1, -1).astype(_F32)

    params = pltpu.CompilerParams(dimension_semantics=("parallel",), vmem_limit_bytes=VMEM_LIMIT_BYTES)

    proj_out_widths = [sbw, sbw, sbw, HEADS * LANES, HEADS * LANES, HEADS * LANES, 2 * D_MODEL]
    qsb, ksb, vsb, qm, km, vm, gates = pl.pallas_call(
        _proj_kernel,
        grid=(T // tm,),
        in_specs=[_row_spec(tm, D_MODEL), _row_spec(tm, 1), _const_spec((8, LANES)), _const_spec((1, D_MODEL)),
                  _const_spec((D_MODEL, sbw)), _const_spec((D_MODEL, sbw)), _const_spec((D_MODEL, sbw)),
                  _const_spec((D_MODEL, Q_RANK)), _const_spec((D_MODEL, KV_RANK + LANES)),
                  _const_spec((D_MODEL, 2 * D_MODEL)), _const_spec((1, 2 * D_MODEL)),
                  _const_spec((1, Q_RANK)), _const_spec((Q_RANK, HEADS * LANES)),
                  _const_spec((1, KV_RANK)), _const_spec((KV_RANK, HEADS * LANES + sbw))],
        out_specs=[_row_spec(tm, w) for w in proj_out_widths],
        out_shape=[jax.ShapeDtypeStruct((T, w), _BF16) for w in proj_out_widths],
        compiler_params=params,
        name="proj",
    )(x2, pos, _rope_table(), row(norm_mix_pre[0]), wq, wk, wv, wcq, wckr, wgate, row(b_gate[0]),
      row(q_norm[0]), wuq, row(kv_norm[0]), wukv)

    pairs = HEADS // 2
    attn_params = pltpu.CompilerParams(dimension_semantics=("parallel", "parallel", "arbitrary"),
                                       vmem_limit_bytes=VMEM_LIMIT_BYTES)

    def q_spec(t, w):
        return pl.BlockSpec((1, t, w), lambda b, p, i: (b, i, p))

    def kv_spec(w):
        return pl.BlockSpec((1, S, w), lambda b, p, i: (b, 0, p))

    tq, tq_s, tk_s, tk_m = MLA_Q_BLOCK, SB_Q_BLOCK, SB_K_BLOCK, MLA_K_BLOCK
    chains = tq // tq_s
    tri = (lax.broadcasted_iota(jnp.int32, (tk_s, tk_s), 0)
           > lax.broadcasted_iota(jnp.int32, (tk_s, tk_s), 1)).astype(_BF16)
    osb, om = pl.pallas_call(
        functools.partial(_attn_kernel, tq=tq, tk_s=tk_s, tk_m=tk_m),
        grid=(B, pairs, S // tq),
        in_specs=[q_spec(tq, PAIR_WIDTH), kv_spec(PAIR_WIDTH), kv_spec(PAIR_WIDTH),
                  pl.BlockSpec((tk_s, tk_s), lambda b, p, i: (0, 0)),
                  q_spec(tq, 2 * LANES), kv_spec(2 * LANES), kv_spec(2 * LANES)],
        out_specs=[q_spec(tq, PAIR_WIDTH), q_spec(tq, PAIR_WIDTH)],
        out_shape=[jax.ShapeDtypeStruct((B, S, sbw), _BF16)] * 2,
        scratch_shapes=[pltpu.VMEM((chains, 2 * tq_s, tk_s), _F32), pltpu.VMEM((chains, 2 * tq_s, tk_s), _BF16),
                        pltpu.VMEM((chains, 2 * tq_s, 1), _F32), pltpu.VMEM((chains, 2 * tq_s, PAIR_WIDTH), _F32),
                        pltpu.VMEM((2 * tq, tk_m), _F32), pltpu.VMEM((2 * tq, 1), _F32),
                        pltpu.VMEM((2 * tq, 2 * LANES), _F32)],
        compiler_params=attn_params,
        name="attn",
    )(qsb.reshape(B, S, sbw), ksb.reshape(B, S, sbw), vsb.reshape(B, S, sbw), tri,
      qm.reshape(B, S, HEADS * LANES), km.reshape(B, S, HEADS * LANES), vm.reshape(B, S, HEADS * LANES))

    out = pl.pallas_call(
        _tail_kernel,
        grid=(T // tm,),
        in_specs=[_row_spec(tm, D_MODEL), _row_spec(tm, sbw), _row_spec(tm, sbw), _row_spec(tm, 2 * D_MODEL),
                  _weight_spec((sbw, D_MODEL)), _weight_spec((sbw, D_MODEL)), _weight_spec((D_MODEL, D_MODEL)),
                  _const_spec((1, D_MODEL)), _const_spec((1, D_MODEL)), _weight_spec((D_MODEL, 2 * D_FF)),
                  _weight_spec((D_FF, D_MODEL)), _const_spec((1, D_MODEL))],
        out_specs=_row_spec(tm, D_MODEL),
        out_shape=jax.ShapeDtypeStruct((T, D_MODEL), _F32),
        compiler_params=params,
        name="tail",
    )(x2, osb.reshape(T, sbw), om.reshape(T, sbw), gates, w_proj_sb[0].astype(_BF16),
      w_proj_mla[0].astype(_BF16), w_out[0].astype(_BF16), row(norm_mix_post[0]),
      row(norm_ffn_pre[0]), w_gate_up[0].astype(_BF16), w_down[0].astype(_BF16), row(norm_ffn_post[0]))

    return out.reshape(B, S, D_MODEL)
```

```python
import functools
import math

import numpy as np
import jax
import jax.numpy as jnp
from jax import lax
from jax.experimental import pallas as pl
from jax.experimental.pallas import tpu as pltpu

D_MODEL = 1024
HEADS = 8
HEAD_DIM = 64
ROPE_DIM = 32
Q_RANK = 384
KV_RANK = 256
D_FF = 2816
ROPE_THETA = 10000.0
EPS = 1e-6

LANES = 128
PAIR_WIDTH = 2 * HEAD_DIM
TOKEN_BLOCK = 512
SB_Q_BLOCK = 256
SB_K_BLOCK = 256
MLA_Q_BLOCK = 512
MLA_K_BLOCK = 1024
SB_UNDERFLOW_LOG2 = 160.0
VMEM_LIMIT_BYTES = 56 * 1024 * 1024

_F32 = jnp.float32
_BF16 = jnp.bfloat16
_NEG = float(np.finfo(np.float32).min)
LOG2E = math.log2(math.e)
_SIGN_BIT = np.uint32(0x80000000)


def _dot(a, b):
    return jnp.dot(a, b, preferred_element_type=_F32)


def _dot_nt(a, b):
    return lax.dot_general(a, b, (((1,), (1,)), ((), ())), preferred_element_type=_F32)


def _rms(x, g):
    return x * lax.rsqrt(jnp.mean(x * x, axis=-1, keepdims=True) + EPS) * g


def _proj_kernel(x_ref, pos_ref, tab_ref, gpre_ref, wq_ref, wk_ref, wv_ref, wlat_ref,
                 wgate_ref, bgate_ref, qn_ref, wuq_ref, kvn_ref, wukv_ref,
                 qsb_ref, ksb_ref, vsb_ref, qm_ref, km_ref, vm_ref, gates_ref):
    h = _rms(x_ref[...], gpre_ref[...]).astype(_BF16)
    qsb_ref[...] = (_dot(h, wq_ref[...]) * (LOG2E / math.sqrt(HEAD_DIM))).astype(_BF16)
    ksb_ref[...] = _dot(h, wk_ref[...]).astype(_BF16)
    vsb_ref[...] = _dot(h, wv_ref[...]).astype(_BF16)
    gates_ref[...] = jax.nn.sigmoid(_dot(h, wgate_ref[...]) + bgate_ref[...]).astype(_BF16)

    ang = pos_ref[...] * tab_ref[0:1, :]
    cos = jnp.cos(ang)
    sin = jnp.sin(ang)
    s_lo = sin * tab_ref[1:2, :]
    s_hi = sin * tab_ref[2:3, :]

    def rope(v):
        return (v * cos + pltpu.roll(v, LANES - ROPE_DIM // 2, 1) * s_lo
                + pltpu.roll(v, ROPE_DIM // 2, 1) * s_hi)

    lat = _dot(h, wlat_ref[...])
    qm = _dot(_rms(lat[:, :Q_RANK], qn_ref[...]).astype(_BF16), wuq_ref[...])
    kv = _dot(_rms(lat[:, Q_RANK:Q_RANK + KV_RANK], kvn_ref[...]).astype(_BF16), wukv_ref[...])
    ones = jnp.ones((kv.shape[0], LANES), _BF16)
    for pr in range(HEADS // 2):
        v_pair = kv[:, HEADS * LANES + pr * PAIR_WIDTH:HEADS * LANES + (pr + 1) * PAIR_WIDTH]
        vm_ref[:, 2 * pr * LANES:(2 * pr + 1) * LANES] = v_pair.astype(_BF16)
        vm_ref[:, (2 * pr + 1) * LANES:(2 * pr + 2) * LANES] = ones
    kr = rope(lat[:, Q_RANK + KV_RANK:])
    q_scale = LOG2E / math.sqrt(HEAD_DIM + ROPE_DIM)
    for hd in range(HEADS):
        sl = slice(hd * LANES, (hd + 1) * LANES)
        qm_ref[:, sl] = (rope(qm[:, sl]) * q_scale).astype(_BF16)
        km_ref[:, sl] = (kv[:, sl] + kr).astype(_BF16)


def _stack_heads(q_pair, split):
    lane = lax.broadcasted_iota(jnp.int32, (1, q_pair.shape[1]), 1)
    zero = jnp.zeros_like(q_pair)
    return jnp.concatenate([jnp.where(lane < split, q_pair, zero), jnp.where(lane >= split, q_pair, zero)], axis=0)


def _unstack_heads(o, tq):
    lane = lax.broadcasted_iota(jnp.int32, (1, PAIR_WIDTH), 1)
    return jnp.where(lane < HEAD_DIM, o[:tq], o[tq:])


def _attn_kernel(qs_ref, ks_ref, vs_ref, tri_ref, qm_ref, km_ref, vm_ref, os_ref, om_ref,
                 u_scr, spb_scr, tot_scr, acc_s_scr, s_scr, tmax_scr, acc_m_scr, *, tq, tk_s, tk_m):
    g = pl.program_id(2)
    tq_s = SB_Q_BLOCK
    chains = tq // tq_s
    tri = tri_ref[...]
    row_s = lax.broadcasted_iota(jnp.int32, (tq_s, 1), 0)
    kiota_s = lax.broadcasted_iota(jnp.int32, (1, tk_s), 1)
    n_diag = [(g * tq + c * tq_s) // tk_s for c in range(chains)]
    qpos_s = [g * tq + c * tq_s + jnp.concatenate([row_s, row_s], axis=0) for c in range(chains)]
    q_s = [_stack_heads(qs_ref[0, c * tq_s:(c + 1) * tq_s, :], HEAD_DIM) for c in range(chains)]

    def rows_s(j):
        return pl.ds(pl.multiple_of(j * tk_s, tk_s), tk_s)

    def sb_score(c, j, masked):
        z = _dot_nt(q_s[c], ks_ref[0, rows_s(j), :])
        neg_abs = lax.bitcast_convert_type(lax.bitcast_convert_type(z, jnp.uint32) | _SIGN_BIT, _F32)
        sp = jnp.maximum(z, 0.0) + jnp.log(1.0 + jnp.exp2(neg_abs)) * LOG2E
        u = z - sp
        if masked:
            valid = (j * tk_s + kiota_s) < qpos_s[c]
            sp = jnp.where(valid, sp, 0.0)
            u = jnp.where(valid, u, _NEG)
        u_scr[c] = u
        spb_scr[c] = sp.astype(_BF16)
        tot_scr[c] = jnp.sum(sp, axis=1, keepdims=True)

    def sb_value(c, j, carry):
        spb = spb_scr[c]
        later = _dot(spb, tri)
        a = jnp.exp2(u_scr[c] - (later + carry))
        acc_s_scr[c] += _dot(a.astype(_BF16), vs_ref[0, rows_s(j), :])
        return carry + (later[:, 0:1] + spb[:, 0:1].astype(_F32))

    def sb_more(t, carries):
        flags = [jnp.logical_and(n_diag[c] - t >= 1, jnp.min(carries[c] + tot_scr[c]) < SB_UNDERFLOW_LOG2)
                 for c in range(chains)]
        return functools.reduce(jnp.logical_or, flags).astype(jnp.int32)

    def sb_step(t, carries):
        new_carries = []
        for c in range(chains):
            j = jnp.maximum(n_diag[c] - t, 0)
            carry = sb_value(c, j, carries[c])
            sb_score(c, jnp.maximum(j - 1, 0), False)
            new_carries.append(jnp.where(n_diag[c] - t <= 0, jnp.inf, carry))
        new_carries = tuple(new_carries)
        return sb_more(t + 1, new_carries), new_carries

    n_m = (g * tq) // tk_m
    row_m = lax.broadcasted_iota(jnp.int32, (tq, 1), 0)
    qpos_m = g * tq + jnp.concatenate([row_m, row_m], axis=0)
    kiota_m = lax.broadcasted_iota(jnp.int32, (1, tk_m), 1)
    q_m = _stack_heads(qm_ref[0], LANES)

    def rows_m(j):
        return pl.ds(pl.multiple_of(j * tk_m, tk_m), tk_m)

    def mla_score(j, masked):
        s = _dot_nt(q_m, km_ref[0, rows_m(j), :])
        if masked:
            s = jnp.where((j * tk_m + kiota_m) <= qpos_m, s, _NEG)
        s_scr[...] = s
        tmax_scr[...] = jnp.max(s, axis=1, keepdims=True)

    def mla_value(j, m):
        m_new = jnp.maximum(m, tmax_scr[...])
        p = jnp.exp2(s_scr[...] - m_new)
        alpha = jnp.exp2(m - m_new)
        acc_m_scr[...] = alpha * acc_m_scr[...] + _dot(p.astype(_BF16), vm_ref[0, rows_m(j), :])
        return m_new

    def mla_step(t, m):
        j = n_m - t
        m = mla_value(j, m)
        mla_score(j - 1, False)
        return m

    acc_s_scr[...] = jnp.zeros_like(acc_s_scr)
    acc_m_scr[...] = jnp.zeros_like(acc_m_scr)
    mla_score(n_m, True)
    for c in range(chains):
        sb_score(c, n_diag[c], True)

    def both_cond(st):
        t, more, _, _ = st
        return jnp.logical_and(t < n_m, more > 0)

    def both_body(st):
        t, _, carries, m = st
        m = mla_step(t, m)
        more, carries = sb_step(t, carries)
        return t + 1, more, carries, m

    carries = tuple(jnp.zeros((2 * tq_s, 1), _F32) for _ in range(chains))
    m = jnp.full((2 * tq, 1), _NEG, _F32)
    t0 = jnp.int32(0)
    t, more, carries, m = lax.while_loop(both_cond, both_body, (t0, sb_more(t0, carries), carries, m))

    m = lax.fori_loop(t, n_m, mla_step, m)

    def sb_body(st):
        t, _, carries = st
        more, carries = sb_step(t, carries)
        return t + 1, more, carries

    t, _, carries = lax.while_loop(lambda st: st[1] > 0, sb_body, (t, more, carries))

    mla_value(0, m)
    for c in range(chains):
        sb_value(c, jnp.maximum(n_diag[c] - t, 0), carries[c])

    for c in range(chains):
        os_ref[0, c * tq_s:(c + 1) * tq_s, :] = _unstack_heads(acc_s_scr[c], tq_s).astype(_BF16)
    acc = acc_m_scr[...]
    om_ref[0] = _unstack_heads(acc[:, :PAIR_WIDTH] / acc[:, PAIR_WIDTH:], tq).astype(_BF16)


def _tail_kernel(x_ref, osb_ref, om_ref, gates_ref, wpsb_ref, wpm_ref, wout_ref, gmix_ref,
                 gpre_ref, wgu_ref, wdown_ref, gpost_ref, out_ref):
    ysb = _dot(osb_ref[...], wpsb_ref[...])
    ym = _dot(om_ref[...], wpm_ref[...])
    g = gates_ref[...].astype(_F32)
    merged = g[:, :D_MODEL] * ysb + g[:, D_MODEL:] * ym
    y = _dot(merged.astype(_BF16), wout_ref[...])
    x1 = x_ref[...] + _rms(y, gmix_ref[...])

    h = _rms(x1, gpre_ref[...]).astype(_BF16)
    g = _dot(h, wgu_ref[:, :D_FF])
    u = _dot(h, wgu_ref[:, D_FF:])
    f = _dot((g * jax.nn.sigmoid(g) * u).astype(_BF16), wdown_ref[...])
    out_ref[...] = x1 + _rms(f, gpost_ref[...])


def _const_spec(shape):
    return pl.BlockSpec(shape, lambda *_: (0,) * len(shape))


def _weight_spec(shape):
    return pl.BlockSpec(shape, lambda *_: (0,) * len(shape), pipeline_mode=pl.Buffered(1))


def _row_spec(tm, width):
    return pl.BlockSpec((tm, width), lambda t: (t, 0))


def _rope_table():
    inv_freq = ROPE_THETA ** (-jnp.arange(0, ROPE_DIM, 2, dtype=_F32) / ROPE_DIM)
    half = ROPE_DIM // 2
    tab = jnp.zeros((8, LANES), _F32)
    tab = tab.at[0, HEAD_DIM:HEAD_DIM + half].set(inv_freq)
    tab = tab.at[0, HEAD_DIM + half:HEAD_DIM + ROPE_DIM].set(inv_freq)
    tab = tab.at[1, HEAD_DIM:HEAD_DIM + half].set(-1.0)
    tab = tab.at[2, HEAD_DIM + half:HEAD_DIM + ROPE_DIM].set(1.0)
    return tab


def kernel(x, positions, norm_mix_pre, norm_mix_post, w_in, b_gate, q_norm, w_uq, kv_norm, w_ukv,
           w_proj_sb, w_proj_mla, w_out, norm_ffn_pre, norm_ffn_post, w_gate_up, w_down):
    B, S, _ = x.shape
    T = B * S
    tm = min(TOKEN_BLOCK, T)
    assert w_in.shape[0] == 1, "single-layer block"
    assert T % tm == 0
    for tq, tk in ((SB_Q_BLOCK, SB_K_BLOCK), (MLA_Q_BLOCK, MLA_K_BLOCK)):
        assert S % tk == 0 and tk % tq == 0
    assert MLA_Q_BLOCK % SB_Q_BLOCK == 0 and S % MLA_Q_BLOCK == 0

    sbw = HEADS * HEAD_DIM
    win = w_in[0]
    o_cq = 3 * sbw
    o_ckv = o_cq + Q_RANK
    o_kr = o_ckv + KV_RANK
    o_gate = o_kr + ROPE_DIM
    wq = win[:, 0:sbw].astype(_BF16)
    wk = win[:, sbw:2 * sbw].astype(_BF16)
    wv = win[:, 2 * sbw:3 * sbw].astype(_BF16)
    kr_cols = jnp.zeros((D_MODEL, LANES), _F32).at[:, HEAD_DIM:HEAD_DIM + ROPE_DIM].set(win[:, o_kr:o_gate])
    wlat = jnp.concatenate([win[:, o_cq:o_kr], kr_cols], axis=1).astype(_BF16)
    wgate = win[:, o_gate:].astype(_BF16)
    wuq = jnp.pad(w_uq[0].reshape(Q_RANK, HEADS, HEAD_DIM + ROPE_DIM),
                  ((0, 0), (0, 0), (0, LANES - HEAD_DIM - ROPE_DIM))).reshape(Q_RANK, HEADS * LANES).astype(_BF16)
    wukv3 = w_ukv[0].reshape(KV_RANK, HEADS, 2 * HEAD_DIM)
    wuk = jnp.pad(wukv3[:, :, :HEAD_DIM], ((0, 0), (0, 0), (0, LANES - HEAD_DIM))).reshape(KV_RANK, HEADS * LANES)
    wuv = wukv3[:, :, HEAD_DIM:].reshape(KV_RANK, HEADS * HEAD_DIM)
    wukv = jnp.concatenate([wuk, wuv], axis=1).astype(_BF16)

    x2 = x.reshape(T, D_MODEL)
    pos = positions.reshape(T, 1).astype(_F32)
    row = lambda v: v.reshape(1, -1).astype(_F32)

    params = pltpu.CompilerParams(dimension_semantics=("parallel",), vmem_limit_bytes=VMEM_LIMIT_BYTES)

    proj_out_widths = [sbw, sbw, sbw, HEADS * LANES, HEADS * LANES, HEADS * LANES, 2 * D_MODEL]
    qsb, ksb, vsb, qm, km, vm, gates = pl.pallas_call(
        _proj_kernel,
        grid=(T // tm,),
        in_specs=[_row_spec(tm, D_MODEL), _row_spec(tm, 1), _const_spec((8, LANES)), _const_spec((1, D_MODEL)),
                  _const_spec((D_MODEL, sbw)), _const_spec((D_MODEL, sbw)), _const_spec((D_MODEL, sbw)),
                  _const_spec((D_MODEL, Q_RANK + KV_RANK + LANES)),
                  _const_spec((D_MODEL, 2 * D_MODEL)), _const_spec((1, 2 * D_MODEL)),
                  _const_spec((1, Q_RANK)), _const_spec((Q_RANK, HEADS * LANES)),
                  _const_spec((1, KV_RANK)), _const_spec((KV_RANK, HEADS * LANES + sbw))],
        out_specs=[_row_spec(tm, w) for w in proj_out_widths],
        out_shape=[jax.ShapeDtypeStruct((T, w), _BF16) for w in proj_out_widths],
        compiler_params=params,
        name="proj",
    )(x2, pos, _rope_table(), row(norm_mix_pre[0]), wq, wk, wv, wlat, wgate, row(b_gate[0]),
      row(q_norm[0]), wuq, row(kv_norm[0]), wukv)

    pairs = HEADS // 2
    attn_params = pltpu.CompilerParams(dimension_semantics=("parallel", "parallel", "arbitrary"),
                                       vmem_limit_bytes=VMEM_LIMIT_BYTES)

    def q_spec(t, w):
        return pl.BlockSpec((1, t, w), lambda b, p, i: (b, i, p))

    def kv_spec(w):
        return pl.BlockSpec((1, S, w), lambda b, p, i: (b, 0, p))

    tq, tq_s, tk_s, tk_m = MLA_Q_BLOCK, SB_Q_BLOCK, SB_K_BLOCK, MLA_K_BLOCK
    chains = tq // tq_s
    tri = (lax.broadcasted_iota(jnp.int32, (tk_s, tk_s), 0)
           > lax.broadcasted_iota(jnp.int32, (tk_s, tk_s), 1)).astype(_BF16)
    osb, om = pl.pallas_call(
        functools.partial(_attn_kernel, tq=tq, tk_s=tk_s, tk_m=tk_m),
        grid=(B, pairs, S // tq),
        in_specs=[q_spec(tq, PAIR_WIDTH), kv_spec(PAIR_WIDTH), kv_spec(PAIR_WIDTH),
                  pl.BlockSpec((tk_s, tk_s), lambda b, p, i: (0, 0)),
                  q_spec(tq, 2 * LANES), kv_spec(2 * LANES), kv_spec(2 * LANES)],
        out_specs=[q_spec(tq, PAIR_WIDTH), q_spec(tq, PAIR_WIDTH)],
        out_shape=[jax.ShapeDtypeStruct((B, S, sbw), _BF16)] * 2,
        scratch_shapes=[pltpu.VMEM((chains, 2 * tq_s, tk_s), _F32), pltpu.VMEM((chains, 2 * tq_s, tk_s), _BF16),
                        pltpu.VMEM((chains, 2 * tq_s, 1), _F32), pltpu.VMEM((chains, 2 * tq_s, PAIR_WIDTH), _F32),
                        pltpu.VMEM((2 * tq, tk_m), _F32), pltpu.VMEM((2 * tq, 1), _F32),
                        pltpu.VMEM((2 * tq, 2 * LANES), _F32)],
        compiler_params=attn_params,
        name="attn",
    )(qsb.reshape(B, S, sbw), ksb.reshape(B, S, sbw), vsb.reshape(B, S, sbw), tri,
      qm.reshape(B, S, HEADS * LANES), km.reshape(B, S, HEADS * LANES), vm.reshape(B, S, HEADS * LANES))

    out = pl.pallas_call(
        _tail_kernel,
        grid=(T // tm,),
        in_specs=[_row_spec(tm, D_MODEL), _row_spec(tm, sbw), _row_spec(tm, sbw), _row_spec(tm, 2 * D_MODEL),
                  _weight_spec((sbw, D_MODEL)), _weight_spec((sbw, D_MODEL)), _weight_spec((D_MODEL, D_MODEL)),
                  _const_spec((1, D_MODEL)), _const_spec((1, D_MODEL)), _weight_spec((D_MODEL, 2 * D_FF)),
                  _weight_spec((D_FF, D_MODEL)), _const_spec((1, D_MODEL))],
        out_specs=_row_spec(tm, D_MODEL),
        out_shape=jax.ShapeDtypeStruct((T, D_MODEL), _F32),
        compiler_params=params,
        name="tail",
    )(x2, osb.reshape(T, sbw), om.reshape(T, sbw), gates, w_proj_sb[0].astype(_BF16),
      w_proj_mla[0].astype(_BF16), w_out[0].astype(_BF16), row(norm_mix_post[0]),
      row(norm_ffn_pre[0]), w_gate_up[0].astype(_BF16), w_down[0].astype(_BF16), row(norm_ffn_post[0]))

    return out.reshape(B, S, D_MODEL)
```

```python
import functools
import math

import numpy as np
import jax
import jax.numpy as jnp
from jax import lax
from jax.experimental import pallas as pl
from jax.experimental.pallas import tpu as pltpu

D_MODEL = 1024
HEADS = 8
HEAD_DIM = 64
ROPE_DIM = 32
Q_RANK = 384
KV_RANK = 256
D_FF = 2816
ROPE_THETA = 10000.0
EPS = 1e-6

LANES = 128
PAIR_WIDTH = 2 * HEAD_DIM
TOKEN_BLOCK = 512
SB_Q_BLOCK = 256
SB_K_BLOCK = 256
MLA_Q_BLOCK = 512
MLA_K_BLOCK = 1024
SB_UNDERFLOW_LOG2 = 160.0
VMEM_LIMIT_BYTES = 56 * 1024 * 1024

_F32 = jnp.float32
_BF16 = jnp.bfloat16
_NEG = float(np.finfo(np.float32).min)
LOG2E = math.log2(math.e)
_SIGN_BIT = np.uint32(0x80000000)


def _dot(a, b):
    return jnp.dot(a, b, preferred_element_type=_F32)


def _dot_nt(a, b):
    return lax.dot_general(a, b, (((1,), (1,)), ((), ())), preferred_element_type=_F32)


def _rms(x, g):
    return x * lax.rsqrt(jnp.mean(x * x, axis=-1, keepdims=True) + EPS) * g


def _proj_kernel(x_ref, pos_ref, tab_ref, gpre_ref, wq_ref, wk_ref, wv_ref, wlat_ref,
                 wgate_ref, bgate_ref, qn_ref, wuq_ref, kvn_ref, wukv_ref,
                 qsb_ref, ksb_ref, vsb_ref, qm_ref, km_ref, vm_ref, gates_ref):
    h = _rms(x_ref[...], gpre_ref[...]).astype(_BF16)
    qsb_ref[...] = (_dot(h, wq_ref[...]) * (LOG2E / math.sqrt(HEAD_DIM))).astype(_BF16)
    ksb_ref[...] = _dot(h, wk_ref[...]).astype(_BF16)
    vsb_ref[...] = _dot(h, wv_ref[...]).astype(_BF16)
    gates_ref[...] = jax.nn.sigmoid(_dot(h, wgate_ref[...]) + bgate_ref[...]).astype(_BF16)

    ang = pos_ref[...] * tab_ref[0:1, :]
    cos = jnp.cos(ang)
    sin = jnp.sin(ang)
    s_lo = sin * tab_ref[1:2, :]
    s_hi = sin * tab_ref[2:3, :]

    def rope(v):
        return (v * cos + pltpu.roll(v, LANES - ROPE_DIM // 2, 1) * s_lo
                + pltpu.roll(v, ROPE_DIM // 2, 1) * s_hi)

    lat = _dot(h, wlat_ref[...])
    qm = _dot(_rms(lat[:, :Q_RANK], qn_ref[...]).astype(_BF16), wuq_ref[...])
    kv = _dot(_rms(lat[:, Q_RANK:Q_RANK + KV_RANK], kvn_ref[...]).astype(_BF16), wukv_ref[...])
    ones = jnp.ones((kv.shape[0], LANES), _BF16)
    for pr in range(HEADS // 2):
        v_pair = kv[:, HEADS * LANES + pr * PAIR_WIDTH:HEADS * LANES + (pr + 1) * PAIR_WIDTH]
        vm_ref[:, 2 * pr * LANES:(2 * pr + 1) * LANES] = v_pair.astype(_BF16)
        vm_ref[:, (2 * pr + 1) * LANES:(2 * pr + 2) * LANES] = ones
    kr = rope(lat[:, Q_RANK + KV_RANK:])
    q_scale = LOG2E / math.sqrt(HEAD_DIM + ROPE_DIM)
    for hd in range(HEADS):
        sl = slice(hd * LANES, (hd + 1) * LANES)
        qm_ref[:, sl] = (rope(qm[:, sl]) * q_scale).astype(_BF16)
        km_ref[:, sl] = (kv[:, sl] + kr).astype(_BF16)


def _stack_heads(q_pair, split):
    lane = lax.broadcasted_iota(jnp.int32, (1, q_pair.shape[1]), 1)
    zero = jnp.zeros_like(q_pair)
    return jnp.concatenate([jnp.where(lane < split, q_pair, zero), jnp.where(lane >= split, q_pair, zero)], axis=0)


def _unstack_heads(o, tq):
    lane = lax.broadcasted_iota(jnp.int32, (1, PAIR_WIDTH), 1)
    return jnp.where(lane < HEAD_DIM, o[:tq], o[tq:])


def _attn_kernel(qs_ref, ks_ref, vs_ref, tri_ref, qm_ref, km_ref, vm_ref, os_ref, om_ref,
                 u_scr, spb_scr, tot_scr, acc_s_scr, s_scr, tmax_scr, acc_m_scr, *, tq, tk_s, tk_m):
    g = pl.program_id(2)
    tq_s = SB_Q_BLOCK
    chains = tq // tq_s
    tri = tri_ref[...]
    row_s = lax.broadcasted_iota(jnp.int32, (tq_s, 1), 0)
    kiota_s = lax.broadcasted_iota(jnp.int32, (1, tk_s), 1)
    n_diag = [(g * tq + c * tq_s) // tk_s for c in range(chains)]
    qpos_s = [g * tq + c * tq_s + jnp.concatenate([row_s, row_s], axis=0) for c in range(chains)]
    q_s = [_stack_heads(qs_ref[0, c * tq_s:(c + 1) * tq_s, :], HEAD_DIM) for c in range(chains)]

    def rows_s(j):
        return pl.ds(pl.multiple_of(j * tk_s, tk_s), tk_s)

    def sb_score(c, j, masked):
        z = _dot_nt(q_s[c], ks_ref[0, rows_s(j), :])
        neg_abs = lax.bitcast_convert_type(lax.bitcast_convert_type(z, jnp.uint32) | _SIGN_BIT, _F32)
        sp = jnp.maximum(z, 0.0) + jnp.log(1.0 + jnp.exp2(neg_abs)) * LOG2E
        u = z - sp
        if masked:
            valid = (j * tk_s + kiota_s) < qpos_s[c]
            sp = jnp.where(valid, sp, 0.0)
            u = jnp.where(valid, u, _NEG)
        u_scr[c] = u
        spb_scr[c] = sp.astype(_BF16)
        tot_scr[c] = jnp.sum(sp, axis=1, keepdims=True)

    def sb_value(c, j, carry):
        spb = spb_scr[c]
        later = _dot(spb, tri)
        a = jnp.exp2(u_scr[c] - (later + carry))
        acc_s_scr[c] += _dot(a.astype(_BF16), vs_ref[0, rows_s(j), :])
        return carry + (later[:, 0:1] + spb[:, 0:1].astype(_F32))

    def sb_more(t, carries):
        flags = [jnp.logical_and(n_diag[c] - t >= 1, jnp.min(carries[c] + tot_scr[c]) < SB_UNDERFLOW_LOG2)
                 for c in range(chains)]
        return functools.reduce(jnp.logical_or, flags).astype(jnp.int32)

    def sb_step(t, carries):
        new_carries = []
        for c in range(chains):
            j = jnp.maximum(n_diag[c] - t, 0)
            carry = sb_value(c, j, carries[c])
            sb_score(c, jnp.maximum(j - 1, 0), False)
            new_carries.append(jnp.where(n_diag[c] - t <= 0, jnp.inf, carry))
        new_carries = tuple(new_carries)
        return sb_more(t + 1, new_carries), new_carries

    n_m = (g * tq) // tk_m
    row_m = lax.broadcasted_iota(jnp.int32, (tq, 1), 0)
    qpos_m = g * tq + jnp.concatenate([row_m, row_m], axis=0)
    kiota_m = lax.broadcasted_iota(jnp.int32, (1, tk_m), 1)
    q_m = _stack_heads(qm_ref[0], LANES)

    def rows_m(j):
        return pl.ds(pl.multiple_of(j * tk_m, tk_m), tk_m)

    def mla_score(j, masked):
        s = _dot_nt(q_m, km_ref[0, rows_m(j), :])
        if masked:
            s = jnp.where((j * tk_m + kiota_m) <= qpos_m, s, _NEG)
        s_scr[...] = s
        tmax_scr[...] = jnp.broadcast_to(jnp.max(s, axis=1, keepdims=True), tmax_scr.shape)

    def mla_value(j, m):
        m_new = jnp.maximum(m, tmax_scr[...])
        p = jnp.exp2(s_scr[...] - jnp.tile(m_new, (1, tk_m // LANES)))
        alpha = jnp.exp2(m - m_new)
        acc_m_scr[...] = jnp.tile(alpha, (1, 2)) * acc_m_scr[...] + _dot(p.astype(_BF16), vm_ref[0, rows_m(j), :])
        return m_new

    def mla_step(t, m):
        j = n_m - t
        m = mla_value(j, m)
        mla_score(j - 1, False)
        return m

    acc_s_scr[...] = jnp.zeros_like(acc_s_scr)
    acc_m_scr[...] = jnp.zeros_like(acc_m_scr)
    mla_score(n_m, True)
    for c in range(chains):
        sb_score(c, n_diag[c], True)

    def both_cond(st):
        t, more, _, _ = st
        return jnp.logical_and(t < n_m, more > 0)

    def both_body(st):
        t, _, carries, m = st
        m = mla_step(t, m)
        more, carries = sb_step(t, carries)
        return t + 1, more, carries, m

    carries = tuple(jnp.zeros((2 * tq_s, 1), _F32) for _ in range(chains))
    m = jnp.full((2 * tq, LANES), _NEG, _F32)
    t0 = jnp.int32(0)
    t, more, carries, m = lax.while_loop(both_cond, both_body, (t0, sb_more(t0, carries), carries, m))

    m = lax.fori_loop(t, n_m, mla_step, m)

    def sb_body(st):
        t, _, carries = st
        more, carries = sb_step(t, carries)
        return t + 1, more, carries

    t, _, carries = lax.while_loop(lambda st: st[1] > 0, sb_body, (t, more, carries))

    mla_value(0, m)
    for c in range(chains):
        sb_value(c, jnp.maximum(n_diag[c] - t, 0), carries[c])

    for c in range(chains):
        os_ref[0, c * tq_s:(c + 1) * tq_s, :] = _unstack_heads(acc_s_scr[c], tq_s).astype(_BF16)
    acc = acc_m_scr[...]
    om_ref[0] = _unstack_heads(acc[:, :PAIR_WIDTH] / acc[:, PAIR_WIDTH:], tq).astype(_BF16)


def _tail_kernel(x_ref, osb_ref, om_ref, gates_ref, wpsb_ref, wpm_ref, wout_ref, gmix_ref,
                 gpre_ref, wgu_ref, wdown_ref, gpost_ref, out_ref):
    ysb = _dot(osb_ref[...], wpsb_ref[...])
    ym = _dot(om_ref[...], wpm_ref[...])
    g = gates_ref[...].astype(_F32)
    merged = g[:, :D_MODEL] * ysb + g[:, D_MODEL:] * ym
    y = _dot(merged.astype(_BF16), wout_ref[...])
    x1 = x_ref[...] + _rms(y, gmix_ref[...])

    h = _rms(x1, gpre_ref[...]).astype(_BF16)
    g = _dot(h, wgu_ref[:, :D_FF])
    u = _dot(h, wgu_ref[:, D_FF:])
    f = _dot((g * jax.nn.sigmoid(g) * u).astype(_BF16), wdown_ref[...])
    out_ref[...] = x1 + _rms(f, gpost_ref[...])


def _const_spec(shape):
    return pl.BlockSpec(shape, lambda *_: (0,) * len(shape))


def _weight_spec(shape):
    return pl.BlockSpec(shape, lambda *_: (0,) * len(shape), pipeline_mode=pl.Buffered(1))


def _row_spec(tm, width):
    return pl.BlockSpec((tm, width), lambda t: (t, 0))


def _rope_table():
    inv_freq = ROPE_THETA ** (-jnp.arange(0, ROPE_DIM, 2, dtype=_F32) / ROPE_DIM)
    half = ROPE_DIM // 2
    tab = jnp.zeros((8, LANES), _F32)
    tab = tab.at[0, HEAD_DIM:HEAD_DIM + half].set(inv_freq)
    tab = tab.at[0, HEAD_DIM + half:HEAD_DIM + ROPE_DIM].set(inv_freq)
    tab = tab.at[1, HEAD_DIM:HEAD_DIM + half].set(-1.0)
    tab = tab.at[2, HEAD_DIM + half:HEAD_DIM + ROPE_DIM].set(1.0)
    return tab


def kernel(x, positions, norm_mix_pre, norm_mix_post, w_in, b_gate, q_norm, w_uq, kv_norm, w_ukv,
           w_proj_sb, w_proj_mla, w_out, norm_ffn_pre, norm_ffn_post, w_gate_up, w_down):
    B, S, _ = x.shape
    T = B * S
    tm = min(TOKEN_BLOCK, T)
    assert w_in.shape[0] == 1, "single-layer block"
    assert T % tm == 0
    for tq, tk in ((SB_Q_BLOCK, SB_K_BLOCK), (MLA_Q_BLOCK, MLA_K_BLOCK)):
        assert S % tk == 0 and tk % tq == 0
    assert MLA_Q_BLOCK % SB_Q_BLOCK == 0 and S % MLA_Q_BLOCK == 0

    sbw = HEADS * HEAD_DIM
    win = w_in[0]
    o_cq = 3 * sbw
    o_ckv = o_cq + Q_RANK
    o_kr = o_ckv + KV_RANK
    o_gate = o_kr + ROPE_DIM
    wq = win[:, 0:sbw].astype(_BF16)
    wk = win[:, sbw:2 * sbw].astype(_BF16)
    wv = win[:, 2 * sbw:3 * sbw].astype(_BF16)
    kr_cols = jnp.zeros((D_MODEL, LANES), _F32).at[:, HEAD_DIM:HEAD_DIM + ROPE_DIM].set(win[:, o_kr:o_gate])
    wlat = jnp.concatenate([win[:, o_cq:o_kr], kr_cols], axis=1).astype(_BF16)
    wgate = win[:, o_gate:].astype(_BF16)
    wuq = jnp.pad(w_uq[0].reshape(Q_RANK, HEADS, HEAD_DIM + ROPE_DIM),
                  ((0, 0), (0, 0), (0, LANES - HEAD_DIM - ROPE_DIM))).reshape(Q_RANK, HEADS * LANES).astype(_BF16)
    wukv3 = w_ukv[0].reshape(KV_RANK, HEADS, 2 * HEAD_DIM)
    wuk = jnp.pad(wukv3[:, :, :HEAD_DIM], ((0, 0), (0, 0), (0, LANES - HEAD_DIM))).reshape(KV_RANK, HEADS * LANES)
    wuv = wukv3[:, :, HEAD_DIM:].reshape(KV_RANK, HEADS * HEAD_DIM)
    wukv = jnp.concatenate([wuk, wuv], axis=1).astype(_BF16)

    x2 = x.reshape(T, D_MODEL)
    pos = positions.reshape(T, 1).astype(_F32)
    row = lambda v: v.reshape(1, -1).astype(_F32)

    params = pltpu.CompilerParams(dimension_semantics=("parallel",), vmem_limit_bytes=VMEM_LIMIT_BYTES)

    proj_out_widths = [sbw, sbw, sbw, HEADS * LANES, HEADS * LANES, HEADS * LANES, 2 * D_MODEL]
    qsb, ksb, vsb, qm, km, vm, gates = pl.pallas_call(
        _proj_kernel,
        grid=(T // tm,),
        in_specs=[_row_spec(tm, D_MODEL), _row_spec(tm, 1), _const_spec((8, LANES)), _const_spec((1, D_MODEL)),
                  _const_spec((D_MODEL, sbw)), _const_spec((D_MODEL, sbw)), _const_spec((D_MODEL, sbw)),
                  _const_spec((D_MODEL, Q_RANK + KV_RANK + LANES)),
                  _const_spec((D_MODEL, 2 * D_MODEL)), _const_spec((1, 2 * D_MODEL)),
                  _const_spec((1, Q_RANK)), _const_spec((Q_RANK, HEADS * LANES)),
                  _const_spec((1, KV_RANK)), _const_spec((KV_RANK, HEADS * LANES + sbw))],
        out_specs=[_row_spec(tm, w) for w in proj_out_widths],
        out_shape=[jax.ShapeDtypeStruct((T, w), _BF16) for w in proj_out_widths],
        compiler_params=params,
        name="proj",
    )(x2, pos, _rope_table(), row(norm_mix_pre[0]), wq, wk, wv, wlat, wgate, row(b_gate[0]),
      row(q_norm[0]), wuq, row(kv_norm[0]), wukv)

    pairs = HEADS // 2
    attn_params = pltpu.CompilerParams(dimension_semantics=("parallel", "parallel", "arbitrary"),
                                       vmem_limit_bytes=VMEM_LIMIT_BYTES)

    def q_spec(t, w):
        return pl.BlockSpec((1, t, w), lambda b, p, i: (b, i, p))

    def kv_spec(w):
        return pl.BlockSpec((1, S, w), lambda b, p, i: (b, 0, p))

    tq, tq_s, tk_s, tk_m = MLA_Q_BLOCK, SB_Q_BLOCK, SB_K_BLOCK, MLA_K_BLOCK
    chains = tq // tq_s
    tri = (lax.broadcasted_iota(jnp.int32, (tk_s, tk_s), 0)
           > lax.broadcasted_iota(jnp.int32, (tk_s, tk_s), 1)).astype(_BF16)
    osb, om = pl.pallas_call(
        functools.partial(_attn_kernel, tq=tq, tk_s=tk_s, tk_m=tk_m),
        grid=(B, pairs, S // tq),
        in_specs=[q_spec(tq, PAIR_WIDTH), kv_spec(PAIR_WIDTH), kv_spec(PAIR_WIDTH),
                  pl.BlockSpec((tk_s, tk_s), lambda b, p, i: (0, 0)),
                  q_spec(tq, 2 * LANES), kv_spec(2 * LANES), kv_spec(2 * LANES)],
        out_specs=[q_spec(tq, PAIR_WIDTH), q_spec(tq, PAIR_WIDTH)],
        out_shape=[jax.ShapeDtypeStruct((B, S, sbw), _BF16)] * 2,
        scratch_shapes=[pltpu.VMEM((chains, 2 * tq_s, tk_s), _F32), pltpu.VMEM((chains, 2 * tq_s, tk_s), _BF16),
                        pltpu.VMEM((chains, 2 * tq_s, 1), _F32), pltpu.VMEM((chains, 2 * tq_s, PAIR_WIDTH), _F32),
                        pltpu.VMEM((2 * tq, tk_m), _F32), pltpu.VMEM((2 * tq, LANES), _F32),
                        pltpu.VMEM((2 * tq, 2 * LANES), _F32)],
        compiler_params=attn_params,
        name="attn",
    )(qsb.reshape(B, S, sbw), ksb.reshape(B, S, sbw), vsb.reshape(B, S, sbw), tri,
      qm.reshape(B, S, HEADS * LANES), km.reshape(B, S, HEADS * LANES), vm.reshape(B, S, HEADS * LANES))

    out = pl.pallas_call(
        _tail_kernel,
        grid=(T // tm,),
        in_specs=[_row_spec(tm, D_MODEL), _row_spec(tm, sbw), _row_spec(tm, sbw), _row_spec(tm, 2 * D_MODEL),
                  _weight_spec((sbw, D_MODEL)), _weight_spec((sbw, D_MODEL)), _weight_spec((D_MODEL, D_MODEL)),
                  _const_spec((1, D_MODEL)), _const_spec((1, D_MODEL)), _weight_spec((D_MODEL, 2 * D_FF)),
                  _weight_spec((D_FF, D_MODEL)), _const_spec((1, D_MODEL))],
        out_specs=_row_spec(tm, D_MODEL),
        out_shape=jax.ShapeDtypeStruct((T, D_MODEL), _F32),
        compiler_params=params,
        name="tail",
    )(x2, osb.reshape(T, sbw), om.reshape(T, sbw), gates, w_proj_sb[0].astype(_BF16),
      w_proj_mla[0].astype(_BF16), w_out[0].astype(_BF16), row(norm_mix_post[0]),
      row(norm_ffn_pre[0]), w_gate_up[0].astype(_BF16), w_down[0].astype(_BF16), row(norm_ffn_post[0]))

    return out.reshape(B, S, D_MODEL)
```

```python
import functools
import math

import numpy as np
import jax
import jax.numpy as jnp
from jax import lax
from jax.experimental import pallas as pl
from jax.experimental.pallas import tpu as pltpu

D_MODEL = 1024
HEADS = 8
HEAD_DIM = 64
ROPE_DIM = 32
Q_RANK = 384
KV_RANK = 256
D_FF = 2816
ROPE_THETA = 10000.0
EPS = 1e-6

LANES = 128
PAIR_WIDTH = 2 * HEAD_DIM
TOKEN_BLOCK = 512
SB_Q_BLOCK = 256
SB_K_BLOCK = 256
MLA_Q_BLOCK = 512
MLA_K_BLOCK = 1024
SB_UNDERFLOW_LOG2 = 160.0
VMEM_LIMIT_BYTES = 56 * 1024 * 1024

_F32 = jnp.float32
_BF16 = jnp.bfloat16
_NEG = float(np.finfo(np.float32).min)
LOG2E = math.log2(math.e)
_SIGN_BIT = np.uint32(0x80000000)


def _dot(a, b):
    return jnp.dot(a, b, preferred_element_type=_F32)


def _dot_nt(a, b):
    return lax.dot_general(a, b, (((1,), (1,)), ((), ())), preferred_element_type=_F32)


def _rms(x, g):
    return x * lax.rsqrt(jnp.mean(x * x, axis=-1, keepdims=True) + EPS) * g


def _proj_kernel(x_ref, pos_ref, tab_ref, gpre_ref, wq_ref, wk_ref, wv_ref, wlat_ref,
                 wgate_ref, bgate_ref, qn_ref, wuq_ref, kvn_ref, wukv_ref,
                 qsb_ref, ksb_ref, vsb_ref, qm_ref, km_ref, vm_ref, gates_ref):
    h = _rms(x_ref[...], gpre_ref[...]).astype(_BF16)
    qsb_ref[...] = (_dot(h, wq_ref[...]) * (LOG2E / math.sqrt(HEAD_DIM))).astype(_BF16)
    ksb_ref[...] = _dot(h, wk_ref[...]).astype(_BF16)
    vsb_ref[...] = _dot(h, wv_ref[...]).astype(_BF16)
    gates_ref[...] = jax.nn.sigmoid(_dot(h, wgate_ref[...]) + bgate_ref[...]).astype(_BF16)

    ang = pos_ref[...] * tab_ref[0:1, :]
    cos = jnp.cos(ang)
    sin = jnp.sin(ang)
    s_lo = sin * tab_ref[1:2, :]
    s_hi = sin * tab_ref[2:3, :]

    def rope(v):
        return (v * cos + pltpu.roll(v, LANES - ROPE_DIM // 2, 1) * s_lo
                + pltpu.roll(v, ROPE_DIM // 2, 1) * s_hi)

    lat = _dot(h, wlat_ref[...])
    qm = _dot(_rms(lat[:, :Q_RANK], qn_ref[...]).astype(_BF16), wuq_ref[...])
    kv = _dot(_rms(lat[:, Q_RANK:Q_RANK + KV_RANK], kvn_ref[...]).astype(_BF16), wukv_ref[...])
    ones = jnp.ones((kv.shape[0], LANES), _BF16)
    for pr in range(HEADS // 2):
        v_pair = kv[:, HEADS * LANES + pr * PAIR_WIDTH:HEADS * LANES + (pr + 1) * PAIR_WIDTH]
        vm_ref[:, 2 * pr * LANES:(2 * pr + 1) * LANES] = v_pair.astype(_BF16)
        vm_ref[:, (2 * pr + 1) * LANES:(2 * pr + 2) * LANES] = ones
    kr = rope(lat[:, Q_RANK + KV_RANK:])
    q_scale = LOG2E / math.sqrt(HEAD_DIM + ROPE_DIM)
    for hd in range(HEADS):
        sl = slice(hd * LANES, (hd + 1) * LANES)
        qm_ref[:, sl] = (rope(qm[:, sl]) * q_scale).astype(_BF16)
        km_ref[:, sl] = (kv[:, sl] + kr).astype(_BF16)


def _stack_heads(q_pair, split):
    lane = lax.broadcasted_iota(jnp.int32, (1, q_pair.shape[1]), 1)
    zero = jnp.zeros_like(q_pair)
    return jnp.concatenate([jnp.where(lane < split, q_pair, zero), jnp.where(lane >= split, q_pair, zero)], axis=0)


def _unstack_heads(o, tq):
    lane = lax.broadcasted_iota(jnp.int32, (1, PAIR_WIDTH), 1)
    return jnp.where(lane < HEAD_DIM, o[:tq], o[tq:])


def _attn_kernel(qs_ref, ks_ref, vs_ref, tri_ref, qm_ref, km_ref, vm_ref, os_ref, om_ref,
                 u_scr, spb_scr, tot_scr, acc_s_scr, s_scr, tmax_scr, acc_m_scr, *, tq, tk_s, tk_m):
    g = pl.program_id(2)
    tq_s = SB_Q_BLOCK
    chains = tq // tq_s
    tri = tri_ref[...]
    row_s = lax.broadcasted_iota(jnp.int32, (tq_s, 1), 0)
    kiota_s = lax.broadcasted_iota(jnp.int32, (1, tk_s), 1)
    n_diag = [(g * tq + c * tq_s) // tk_s for c in range(chains)]
    qpos_s = [g * tq + c * tq_s + jnp.concatenate([row_s, row_s], axis=0) for c in range(chains)]
    q_s = [_stack_heads(qs_ref[0, c * tq_s:(c + 1) * tq_s, :], HEAD_DIM) for c in range(chains)]

    def rows_s(j):
        return pl.ds(pl.multiple_of(j * tk_s, tk_s), tk_s)

    def sb_score(c, j, masked):
        z = _dot_nt(q_s[c], ks_ref[0, rows_s(j), :])
        neg_abs = lax.bitcast_convert_type(lax.bitcast_convert_type(z, jnp.uint32) | _SIGN_BIT, _F32)
        sp = jnp.maximum(z, 0.0) + jnp.log(1.0 + jnp.exp2(neg_abs)) * LOG2E
        u = z - sp
        if masked:
            valid = (j * tk_s + kiota_s) < qpos_s[c]
            sp = jnp.where(valid, sp, 0.0)
            u = jnp.where(valid, u, _NEG)
        u_scr[c] = u
        spb_scr[c] = sp.astype(_BF16)
        tot_scr[c] = jnp.sum(sp, axis=1, keepdims=True)

    def sb_value(c, j, carry):
        spb = spb_scr[c]
        later = _dot(spb, tri)
        a = jnp.exp2(u_scr[c] - (later + carry))
        acc_s_scr[c] += _dot(a.astype(_BF16), vs_ref[0, rows_s(j), :])
        return carry + (later[:, 0:1] + spb[:, 0:1].astype(_F32))

    def sb_more(t, carries):
        flags = [jnp.logical_and(n_diag[c] - t >= 1, jnp.min(carries[c] + tot_scr[c]) < SB_UNDERFLOW_LOG2)
                 for c in range(chains)]
        return functools.reduce(jnp.logical_or, flags).astype(jnp.int32)

    def sb_step(t, carries):
        new_carries = []
        for c in range(chains):
            j = jnp.maximum(n_diag[c] - t, 0)
            carry = sb_value(c, j, carries[c])
            sb_score(c, jnp.maximum(j - 1, 0), False)
            new_carries.append(jnp.where(n_diag[c] - t <= 0, jnp.inf, carry))
        new_carries = tuple(new_carries)
        return sb_more(t + 1, new_carries), new_carries

    n_m = (g * tq) // tk_m
    row_m = lax.broadcasted_iota(jnp.int32, (tq, 1), 0)
    qpos_m = g * tq + jnp.concatenate([row_m, row_m], axis=0)
    kiota_m = lax.broadcasted_iota(jnp.int32, (1, tk_m), 1)
    q_m = _stack_heads(qm_ref[0], LANES)

    def rows_m(j):
        return pl.ds(pl.multiple_of(j * tk_m, tk_m), tk_m)

    def mla_score(j, masked):
        s = _dot_nt(q_m, km_ref[0, rows_m(j), :])
        if masked:
            s = jnp.where((j * tk_m + kiota_m) <= qpos_m, s, _NEG)
        s_scr[...] = s
        tmax_scr[...] = jnp.broadcast_to(jnp.max(s, axis=1, keepdims=True), tmax_scr.shape)

    def mla_value(j, m):
        m_new = jnp.maximum(m, tmax_scr[...])
        p = jnp.exp2(s_scr[...] - jnp.tile(m_new, (1, tk_m // LANES)))
        alpha = jnp.exp2(m - m_new)
        acc_m_scr[...] = jnp.tile(alpha, (1, 2)) * acc_m_scr[...] + _dot(p.astype(_BF16), vm_ref[0, rows_m(j), :])
        return m_new

    def mla_step(t, m):
        j = n_m - t
        m = mla_value(j, m)
        mla_score(j - 1, False)
        return m

    acc_s_scr[...] = jnp.zeros_like(acc_s_scr)
    acc_m_scr[...] = jnp.zeros_like(acc_m_scr)
    for c in range(chains):
        sb_score(c, n_diag[c], True)
    mla_score(n_m, True)

    def both_cond(st):
        t, more, _, _ = st
        return jnp.logical_and(t < n_m, more > 0)

    def both_body(st):
        t, _, carries, m = st
        more, carries = sb_step(t, carries)
        m = mla_step(t, m)
        return t + 1, more, carries, m

    carries = tuple(jnp.zeros((2 * tq_s, 1), _F32) for _ in range(chains))
    m = jnp.full((2 * tq, LANES), _NEG, _F32)
    t0 = jnp.int32(0)
    t, more, carries, m = lax.while_loop(both_cond, both_body, (t0, sb_more(t0, carries), carries, m))

    m = lax.fori_loop(t, n_m, mla_step, m)

    def sb_body(st):
        t, _, carries = st
        more, carries = sb_step(t, carries)
        return t + 1, more, carries

    t, _, carries = lax.while_loop(lambda st: st[1] > 0, sb_body, (t, more, carries))

    mla_value(0, m)
    for c in range(chains):
        sb_value(c, jnp.maximum(n_diag[c] - t, 0), carries[c])

    for c in range(chains):
        os_ref[0, c * tq_s:(c + 1) * tq_s, :] = _unstack_heads(acc_s_scr[c], tq_s).astype(_BF16)
    acc = acc_m_scr[...]
    om_ref[0] = _unstack_heads(acc[:, :PAIR_WIDTH] / acc[:, PAIR_WIDTH:], tq).astype(_BF16)


def _tail_kernel(x_ref, osb_ref, om_ref, gates_ref, wpsb_ref, wpm_ref, wout_ref, gmix_ref,
                 gpre_ref, wgu_ref, wdown_ref, gpost_ref, out_ref):
    ysb = _dot(osb_ref[...], wpsb_ref[...])
    ym = _dot(om_ref[...], wpm_ref[...])
    g = gates_ref[...].astype(_F32)
    merged = g[:, :D_MODEL] * ysb + g[:, D_MODEL:] * ym
    y = _dot(merged.astype(_BF16), wout_ref[...])
    x1 = x_ref[...] + _rms(y, gmix_ref[...])

    h = _rms(x1, gpre_ref[...]).astype(_BF16)
    g = _dot(h, wgu_ref[:, :D_FF])
    u = _dot(h, wgu_ref[:, D_FF:])
    f = _dot((g * jax.nn.sigmoid(g) * u).astype(_BF16), wdown_ref[...])
    out_ref[...] = x1 + _rms(f, gpost_ref[...])


def _const_spec(shape):
    return pl.BlockSpec(shape, lambda *_: (0,) * len(shape))


def _weight_spec(shape):
    return pl.BlockSpec(shape, lambda *_: (0,) * len(shape), pipeline_mode=pl.Buffered(1))


def _row_spec(tm, width):
    return pl.BlockSpec((tm, width), lambda t: (t, 0))


def _rope_table():
    inv_freq = ROPE_THETA ** (-jnp.arange(0, ROPE_DIM, 2, dtype=_F32) / ROPE_DIM)
    half = ROPE_DIM // 2
    tab = jnp.zeros((8, LANES), _F32)
    tab = tab.at[0, HEAD_DIM:HEAD_DIM + half].set(inv_freq)
    tab = tab.at[0, HEAD_DIM + half:HEAD_DIM + ROPE_DIM].set(inv_freq)
    tab = tab.at[1, HEAD_DIM:HEAD_DIM + half].set(-1.0)
    tab = tab.at[2, HEAD_DIM + half:HEAD_DIM + ROPE_DIM].set(1.0)
    return tab


def kernel(x, positions, norm_mix_pre, norm_mix_post, w_in, b_gate, q_norm, w_uq, kv_norm, w_ukv,
           w_proj_sb, w_proj_mla, w_out, norm_ffn_pre, norm_ffn_post, w_gate_up, w_down):
    B, S, _ = x.shape
    T = B * S
    tm = min(TOKEN_BLOCK, T)
    assert w_in.shape[0] == 1, "single-layer block"
    assert T % tm == 0
    for tq, tk in ((SB_Q_BLOCK, SB_K_BLOCK), (MLA_Q_BLOCK, MLA_K_BLOCK)):
        assert S % tk == 0 and tk % tq == 0
    assert MLA_Q_BLOCK % SB_Q_BLOCK == 0 and S % MLA_Q_BLOCK == 0

    sbw = HEADS * HEAD_DIM
    win = w_in[0]
    o_cq = 3 * sbw
    o_ckv = o_cq + Q_RANK
    o_kr = o_ckv + KV_RANK
    o_gate = o_kr + ROPE_DIM
    wq = win[:, 0:sbw].astype(_BF16)
    wk = win[:, sbw:2 * sbw].astype(_BF16)
    wv = win[:, 2 * sbw:3 * sbw].astype(_BF16)
    kr_cols = jnp.zeros((D_MODEL, LANES), _F32).at[:, HEAD_DIM:HEAD_DIM + ROPE_DIM].set(win[:, o_kr:o_gate])
    wlat = jnp.concatenate([win[:, o_cq:o_kr], kr_cols], axis=1).astype(_BF16)
    wgate = win[:, o_gate:].astype(_BF16)
    wuq = jnp.pad(w_uq[0].reshape(Q_RANK, HEADS, HEAD_DIM + ROPE_DIM),
                  ((0, 0), (0, 0), (0, LANES - HEAD_DIM - ROPE_DIM))).reshape(Q_RANK, HEADS * LANES).astype(_BF16)
    wukv3 = w_ukv[0].reshape(KV_RANK, HEADS, 2 * HEAD_DIM)
    wuk = jnp.pad(wukv3[:, :, :HEAD_DIM], ((0, 0), (0, 0), (0, LANES - HEAD_DIM))).reshape(KV_RANK, HEADS * LANES)
    wuv = wukv3[:, :, HEAD_DIM:].reshape(KV_RANK, HEADS * HEAD_DIM)
    wukv = jnp.concatenate([wuk, wuv], axis=1).astype(_BF16)

    x2 = x.reshape(T, D_MODEL)
    pos = positions.reshape(T, 1).astype(_F32)
    row = lambda v: v.reshape(1, -1).astype(_F32)

    params = pltpu.CompilerParams(dimension_semantics=("parallel",), vmem_limit_bytes=VMEM_LIMIT_BYTES)

    proj_out_widths = [sbw, sbw, sbw, HEADS * LANES, HEADS * LANES, HEADS * LANES, 2 * D_MODEL]
    qsb, ksb, vsb, qm, km, vm, gates = pl.pallas_call(
        _proj_kernel,
        grid=(T // tm,),
        in_specs=[_row_spec(tm, D_MODEL), _row_spec(tm, 1), _const_spec((8, LANES)), _const_spec((1, D_MODEL)),
                  _const_spec((D_MODEL, sbw)), _const_spec((D_MODEL, sbw)), _const_spec((D_MODEL, sbw)),
                  _const_spec((D_MODEL, Q_RANK + KV_RANK + LANES)),
                  _const_spec((D_MODEL, 2 * D_MODEL)), _const_spec((1, 2 * D_MODEL)),
                  _const_spec((1, Q_RANK)), _const_spec((Q_RANK, HEADS * LANES)),
                  _const_spec((1, KV_RANK)), _const_spec((KV_RANK, HEADS * LANES + sbw))],
        out_specs=[_row_spec(tm, w) for w in proj_out_widths],
        out_shape=[jax.ShapeDtypeStruct((T, w), _BF16) for w in proj_out_widths],
        compiler_params=params,
        name="proj",
    )(x2, pos, _rope_table(), row(norm_mix_pre[0]), wq, wk, wv, wlat, wgate, row(b_gate[0]),
      row(q_norm[0]), wuq, row(kv_norm[0]), wukv)

    pairs = HEADS // 2
    attn_params = pltpu.CompilerParams(dimension_semantics=("parallel", "parallel", "arbitrary"),
                                       vmem_limit_bytes=VMEM_LIMIT_BYTES)

    def q_spec(t, w):
        return pl.BlockSpec((1, t, w), lambda b, p, i: (b, i, p))

    def kv_spec(w):
        return pl.BlockSpec((1, S, w), lambda b, p, i: (b, 0, p))

    tq, tq_s, tk_s, tk_m = MLA_Q_BLOCK, SB_Q_BLOCK, SB_K_BLOCK, MLA_K_BLOCK
    chains = tq // tq_s
    tri = (lax.broadcasted_iota(jnp.int32, (tk_s, tk_s), 0)
           > lax.broadcasted_iota(jnp.int32, (tk_s, tk_s), 1)).astype(_BF16)
    osb, om = pl.pallas_call(
        functools.partial(_attn_kernel, tq=tq, tk_s=tk_s, tk_m=tk_m),
        grid=(B, pairs, S // tq),
        in_specs=[q_spec(tq, PAIR_WIDTH), kv_spec(PAIR_WIDTH), kv_spec(PAIR_WIDTH),
                  pl.BlockSpec((tk_s, tk_s), lambda b, p, i: (0, 0)),
                  q_spec(tq, 2 * LANES), kv_spec(2 * LANES), kv_spec(2 * LANES)],
        out_specs=[q_spec(tq, PAIR_WIDTH), q_spec(tq, PAIR_WIDTH)],
        out_shape=[jax.ShapeDtypeStruct((B, S, sbw), _BF16)] * 2,
        scratch_shapes=[pltpu.VMEM((chains, 2 * tq_s, tk_s), _F32), pltpu.VMEM((chains, 2 * tq_s, tk_s), _BF16),
                        pltpu.VMEM((chains, 2 * tq_s, 1), _F32), pltpu.VMEM((chains, 2 * tq_s, PAIR_WIDTH), _F32),
                        pltpu.VMEM((2 * tq, tk_m), _F32), pltpu.VMEM((2 * tq, LANES), _F32),
                        pltpu.VMEM((2 * tq, 2 * LANES), _F32)],
        compiler_params=attn_params,
        name="attn",
    )(qsb.reshape(B, S, sbw), ksb.reshape(B, S, sbw), vsb.reshape(B, S, sbw), tri,
      qm.reshape(B, S, HEADS * LANES), km.reshape(B, S, HEADS * LANES), vm.reshape(B, S, HEADS * LANES))

    out = pl.pallas_call(
        _tail_kernel,
        grid=(T // tm,),
        in_specs=[_row_spec(tm, D_MODEL), _row_spec(tm, sbw), _row_spec(tm, sbw), _row_spec(tm, 2 * D_MODEL),
                  _weight_spec((sbw, D_MODEL)), _weight_spec((sbw, D_MODEL)), _weight_spec((D_MODEL, D_MODEL)),
                  _const_spec((1, D_MODEL)), _const_spec((1, D_MODEL)), _weight_spec((D_MODEL, 2 * D_FF)),
                  _weight_spec((D_FF, D_MODEL)), _const_spec((1, D_MODEL))],
        out_specs=_row_spec(tm, D_MODEL),
        out_shape=jax.ShapeDtypeStruct((T, D_MODEL), _F32),
        compiler_params=params,
        name="tail",
    )(x2, osb.reshape(T, sbw), om.reshape(T, sbw), gates, w_proj_sb[0].astype(_BF16),
      w_proj_mla[0].astype(_BF16), w_out[0].astype(_BF16), row(norm_mix_post[0]),
      row(norm_ffn_pre[0]), w_gate_up[0].astype(_BF16), w_down[0].astype(_BF16), row(norm_ffn_post[0]))

    return out.reshape(B, S, D_MODEL)
```
